```python
import jax, jax.numpy as jnp
from jax import lax
import numpy as np

D_MODEL = 2048
BATCH = 1
SEQ = 16384
DEPTH = 2
DEC_BATCH = 4
DEC_SEQ = 8192
PAST_LEN = 128

A_WIDTH = D_MODEL // 4
N_HEADS = 8
HEAD_DIM = D_MODEL // 16
N_KV_HEADS = 2
ATTN_WIDTH = N_HEADS * HEAD_DIM
KV_WIDTH = N_KV_HEADS * HEAD_DIM
C_WIDTH = D_MODEL // 4
D_MIX = A_WIDTH + ATTN_WIDTH + C_WIDTH
IN_SPLITS = (A_WIDTH, A_WIDTH, ATTN_WIDTH, KV_WIDTH, KV_WIDTH, C_WIDTH, C_WIDTH, C_WIDTH)
D_IN = sum(IN_SPLITS)
CONV_A_WIDTH = 31
CONV_C_WIDTH = 3
WINDOW = 128
BLOCK = 128
N_GROUPS = 4
EXPERTS_PER_GROUP = 4
N_EXPERTS = N_GROUPS * EXPERTS_PER_GROUP
TOP_K = 2
D_EXPERT = D_MODEL // 4
EPS = 1e-6
NEG_INF = -1e30

kernel_name = 'hymba_style_hybrid_encoder_hier_moe'


def rmsnorm(x, g):
    xf = x.astype(jnp.float32)
    y = xf * lax.rsqrt(jnp.mean(xf * xf, axis=-1, keepdims=True) + EPS) * g.astype(jnp.float32)
    return y.astype(x.dtype)


def layernorm(x, g, b):
    xf = x.astype(jnp.float32)
    mu = jnp.mean(xf, axis=-1, keepdims=True)
    xc = xf - mu
    var = jnp.mean(xc * xc, axis=-1, keepdims=True)
    y = xc * lax.rsqrt(var + EPS) * g.astype(jnp.float32) + b.astype(jnp.float32)
    return y.astype(x.dtype)


def depthwise_conv(x, w):
    k = w.shape[0]
    pad = (k - 1) // 2
    return lax.conv_general_dilated(x, w[:, None, :].astype(x.dtype), window_strides=(1,),
                                    padding=[(pad, pad)], dimension_numbers=('NWC', 'WIO', 'NWC'),
                                    feature_group_count=x.shape[-1])


def alibi_slopes():
    return 2.0 ** (-8.0 * jnp.arange(1, N_HEADS + 1, dtype=jnp.float32) / N_HEADS)


def windowed_gqa(q, k, v, sink):
    bn, s, _, hd = q.shape
    nb = s // BLOCK
    grp = N_HEADS // N_KV_HEADS
    qb = q.reshape(bn, nb, BLOCK, N_KV_HEADS, grp, hd)
    kp = jnp.pad(k, ((0, 0), (BLOCK, BLOCK), (0, 0), (0, 0)))
    vp = jnp.pad(v, ((0, 0), (BLOCK, BLOCK), (0, 0), (0, 0)))

    def bands(t):
        return jnp.concatenate([t[:, i * BLOCK:i * BLOCK + s].reshape(bn, nb, BLOCK, N_KV_HEADS, hd)
                                for i in range(3)], axis=2)

    kb, vb = bands(kp), bands(vp)
    scores = jnp.einsum('bnqkgd,bnskd->bnkgqs', qb, kb,
                        preferred_element_type=jnp.float32) * (hd ** -0.5)
    q_idx = jnp.arange(BLOCK)[:, None]
    s_idx = jnp.arange(3 * BLOCK)[None, :]
    dist = jnp.abs(s_idx - BLOCK - q_idx)
    kpos = jnp.arange(nb)[:, None] * BLOCK + jnp.arange(3 * BLOCK)[None, :] - BLOCK
    valid = (dist <= WINDOW)[None] & ((kpos >= 0) & (kpos < s))[:, None, :]
    slopes = alibi_slopes().reshape(N_KV_HEADS, grp)
    bias = -slopes[:, :, None, None] * dist.astype(jnp.float32)[None, None]
    logits = jnp.where(valid[None, :, None, None], scores + bias, NEG_INF)
    sink_l = sink.astype(jnp.float32).reshape(N_KV_HEADS, grp)[:, :, None, None]
    m = jnp.maximum(jnp.max(logits, axis=-1, keepdims=True), sink_l)
    p = jnp.exp(logits - m)
    denom = jnp.sum(p, axis=-1, keepdims=True) + jnp.exp(sink_l - m)
    o = jnp.einsum('bnkgqs,bnskd->bnqkgd', p / denom, vb.astype(jnp.float32))
    return o.reshape(bn, s, N_HEADS * hd).astype(q.dtype)


def token_mixer(h, w_in, w_out, conv_a_w, conv_a_b, ln_a_g, ln_a_b, attn_sink, conv_c_w):
    bn, s, _ = h.shape
    z = h @ w_in
    a_val, a_gate, q, k, v, c_x, c_b, c_c = jnp.split(z, np.cumsum(IN_SPLITS)[:-1].tolist(), axis=-1)
    a = a_val * jax.nn.sigmoid(a_gate)
    a = depthwise_conv(a, conv_a_w) + conv_a_b
    a = jax.nn.silu(layernorm(a, ln_a_g, ln_a_b))
    o_b = windowed_gqa(q.reshape(bn, s, N_HEADS, HEAD_DIM),
                       k.reshape(bn, s, N_KV_HEADS, HEAD_DIM),
                       v.reshape(bn, s, N_KV_HEADS, HEAD_DIM), attn_sink)
    o_c = c_b * depthwise_conv(c_c * c_x, conv_c_w)
    return jnp.concatenate([a, o_b, o_c], axis=-1) @ w_out


def hier_moe(h, w_rg, b_rg, w_re, b_re, w_gate, w_up, w_down):
    bn, s, d = h.shape
    t = h.reshape(bn * s, d)
    lg = (t @ w_rg).astype(jnp.float32) + b_rg.astype(jnp.float32)
    pg = jax.nn.softmax(lg, axis=-1)
    g_star = jnp.argmax(lg, axis=-1)
    gp = jnp.take_along_axis(pg, g_star[:, None], axis=-1)
    le = ((t @ w_re).astype(jnp.float32) + b_re.astype(jnp.float32)).reshape(-1, N_GROUPS, EXPERTS_PER_GROUP)
    sel = jnp.take_along_axis(le, g_star[:, None, None], axis=1)[:, 0]
    topv, topi = lax.top_k(jax.nn.softmax(sel, axis=-1), TOP_K)
    topv = topv / jnp.sum(topv, axis=-1, keepdims=True)
    eidx = g_star[:, None] * EXPERTS_PER_GROUP + topi
    combine = gp * jnp.sum(topv[..., None] * jax.nn.one_hot(eidx, N_EXPERTS, dtype=jnp.float32), axis=1)
    y = jnp.zeros((bn * s, d), jnp.float32)
    for e in range(N_EXPERTS):
        he = jax.nn.silu(t @ w_gate[e]) * (t @ w_up[e])
        y = y + combine[:, e:e + 1] * (he @ w_down[e]).astype(jnp.float32)
    return y.astype(h.dtype).reshape(bn, s, d)


def encoder(x, c, norm_mix_g, norm_ffn_g, w_ada, b_ada, w_in, w_out, conv_a_w, conv_a_b, ln_a_g, ln_a_b,
            attn_sink, conv_c_w, w_router_group, b_router_group, w_router_expert, b_router_expert,
            w_gate, w_up, w_down, final_norm_g):
    for l in range(DEPTH):
        mod = jax.nn.silu(c) @ w_ada[l] + b_ada[l]
        sh_m, sc_m, g_m, sh_f, sc_f, g_f = [m[:, None, :] for m in jnp.split(mod, 6, axis=-1)]
        h = rmsnorm(x, norm_mix_g[l]) * (1 + sc_m) + sh_m
        x = x + g_m * token_mixer(h, w_in[l], w_out[l], conv_a_w[l], conv_a_b[l], ln_a_g[l], ln_a_b[l],
                                  attn_sink[l], conv_c_w[l])
        h = rmsnorm(x, norm_ffn_g[l]) * (1 + sc_f) + sh_f
        x = x + g_f * hier_moe(h, w_router_group[l], b_router_group[l], w_router_expert[l],
                               b_router_expert[l], w_gate[l], w_up[l], w_down[l])
    return rmsnorm(x, final_norm_g)


def setup_inputs(seed: int = 0) -> dict:
    key = jax.random.key(seed)
    ks = jax.random.split(key, 26)

    def nrm(k, shape, scale):
        return jax.random.normal(k, shape, jnp.float32) * scale

    D = D_MODEL
    return {
        'x_prompt': nrm(ks[0], (BATCH, SEQ, D), 1.0),
        'x_sample': nrm(ks[1], (DEC_BATCH, DEC_SEQ, D), 1.0),
        'c_prompt': nrm(ks[2], (BATCH, D), 1.0),
        'c_sample': nrm(ks[3], (DEC_BATCH, D), 1.0),
        'norm_mix_g': 1.0 + nrm(ks[4], (DEPTH, D), 0.02),
        'norm_ffn_g': 1.0 + nrm(ks[5], (DEPTH, D), 0.02),
        'w_ada': nrm(ks[6], (DEPTH, D, 6 * D), 0.5 * D ** -0.5),
        'b_ada': nrm(ks[7], (DEPTH, 6 * D), 0.02),
        'w_in': nrm(ks[8], (DEPTH, D, D_IN), D ** -0.5),
        'w_out': nrm(ks[9], (DEPTH, D_MIX, D), D_MIX ** -0.5),
        'conv_a_w': nrm(ks[10], (DEPTH, CONV_A_WIDTH, A_WIDTH), CONV_A_WIDTH ** -0.5),
        'conv_a_b': nrm(ks[11], (DEPTH, A_WIDTH), 0.02),
        'ln_a_g': 1.0 + nrm(ks[12], (DEPTH, A_WIDTH), 0.02),
        'ln_a_b': nrm(ks[13], (DEPTH, A_WIDTH), 0.02),
        'attn_sink': nrm(ks[14], (DEPTH, N_HEADS), 0.5),
        'conv_c_w': nrm(ks[15], (DEPTH, CONV_C_WIDTH, C_WIDTH), CONV_C_WIDTH ** -0.5),
        'w_router_group': nrm(ks[16], (DEPTH, D, N_GROUPS), D ** -0.5),
        'b_router_group': nrm(ks[17], (DEPTH, N_GROUPS), 0.01),
        'w_router_expert': nrm(ks[18], (DEPTH, D, N_EXPERTS), D ** -0.5),
        'b_router_expert': nrm(ks[19], (DEPTH, N_EXPERTS), 0.01),
        'w_gate': nrm(ks[20], (DEPTH, N_EXPERTS, D, D_EXPERT), D ** -0.5),
        'w_up': nrm(ks[21], (DEPTH, N_EXPERTS, D, D_EXPERT), D ** -0.5),
        'w_down': nrm(ks[22], (DEPTH, N_EXPERTS, D_EXPERT, D), D_EXPERT ** -0.5),
        'final_norm_g': 1.0 + nrm(ks[23], (D,), 0.02),
    }


def reference(x_prompt, x_sample, c_prompt, c_sample, norm_mix_g, norm_ffn_g, w_ada, b_ada, w_in, w_out,
              conv_a_w, conv_a_b, ln_a_g, ln_a_b, attn_sink, conv_c_w, w_router_group, b_router_group,
              w_router_expert, b_router_expert, w_gate, w_up, w_down, final_norm_g):
    y_prompt = encoder(x_prompt, c_prompt, norm_mix_g, norm_ffn_g, w_ada, b_ada, w_in, w_out, conv_a_w,
                       conv_a_b, ln_a_g, ln_a_b, attn_sink, conv_c_w, w_router_group, b_router_group,
                       w_router_expert, b_router_expert, w_gate, w_up, w_down, final_norm_g)
    y_sample = encoder(x_sample, c_sample, norm_mix_g, norm_ffn_g, w_ada, b_ada, w_in, w_out, conv_a_w,
                       conv_a_b, ln_a_g, ln_a_b, attn_sink, conv_c_w, w_router_group, b_router_group,
                       w_router_expert, b_router_expert, w_gate, w_up, w_down, final_norm_g)
    return (y_prompt, y_sample)
```

```python
import functools

import jax
import jax.numpy as jnp
import numpy as np
from jax import lax
from jax.experimental import pallas as pl
from jax.experimental.pallas import tpu as pltpu

F32 = jnp.float32
BF16 = jnp.bfloat16

D = 2048
N_HEADS = 8
HEAD_DIM = 128
N_KV = 2
GRP = N_HEADS // N_KV
A_W = 512
C_W = 512
ATT_W = N_HEADS * HEAD_DIM
KV_W = N_KV * HEAD_DIM
D_IN = 4096
CONV_A = 31
CONV_C = 3
WINDOW = 128
BLK = 128
N_GROUPS = 4
EPG = 4
N_EXP = N_GROUPS * EPG
D_EXP = 512
N_PAIR = 6
N_BUCKET = N_GROUPS * N_PAIR
EPS = 1e-6
NEG = -1e30

TM = 256
TME = 512
HALO = 16
PAYLOAD = 128
ROW_W = D + PAYLOAD
ROUTE_ROWS = 8
RROWS = 32
DMA_CHUNK = 2048
VMEM_LIMIT = 56 * 1024 * 1024


def _cparams(sem=("arbitrary",)):
    return pltpu.CompilerParams(dimension_semantics=sem, vmem_limit_bytes=VMEM_LIMIT)


def _resident(shape, index_map):
    return pl.BlockSpec(shape, index_map, pipeline_mode=pl.Buffered(1))


def _ada_body(c_ref, w_ref, b_ref, o_ref):
    c = c_ref[...]
    sc = c * jax.nn.sigmoid(c)
    o_ref[...] = jnp.dot(sc, w_ref[...], preferred_element_type=F32,
                         precision=lax.Precision.HIGHEST) + b_ref[...]


def _ada(c_all, w_ada, b_ada):
    depth = w_ada.shape[0]
    tn = 1024
    return pl.pallas_call(
        _ada_body,
        grid=(depth, 6 * D // tn),
        in_specs=[pl.BlockSpec((8, D), lambda l, j: (0, 0)),
                  pl.BlockSpec((None, D, tn), lambda l, j: (l, 0, j)),
                  pl.BlockSpec((None, 1, tn), lambda l, j: (l, 0, j))],
        out_specs=pl.BlockSpec((None, 8, tn), lambda l, j: (l, 0, j)),
        out_shape=jax.ShapeDtypeStruct((depth, 8, 6 * D), F32),
        compiler_params=_cparams(("arbitrary", "arbitrary")),
    )(c_all, w_ada, b_ada.reshape(depth, 1, 6 * D))


def _seq_of_tile(i, npt, tps):
    return jnp.where(i < npt, 0, 1 + (i - npt) // tps)


def _rms_mod(x, g, scale, shift):
    y = x * lax.rsqrt(jnp.mean(x * x, axis=-1, keepdims=True) + EPS) * g
    return y * (1.0 + scale) + shift


def _pre_body(fused, npt, *refs):
    if fused:
        (xm_ref, y_ref, modp_ref, mod_ref, g_ref, w_ref,
         x_out, a_out, q_out, kv_out, cc_out) = refs
        x = xm_ref[...] + modp_ref[:, 5 * D:6 * D] * y_ref[...]
        x_out[...] = x
    else:
        xp_ref, xs_ref, mod_ref, g_ref, w_ref, a_out, q_out, kv_out, cc_out = refs
        x = jnp.where(pl.program_id(0) < npt, xp_ref[...], xs_ref[...])
    h = _rms_mod(x, g_ref[...], mod_ref[:, D:2 * D], mod_ref[:, 0:D]).astype(BF16)

    za = jnp.dot(h, w_ref[:, 0:2 * A_W], preferred_element_type=F32)
    a_out[...] = (za[:, :A_W] * jax.nn.sigmoid(za[:, A_W:])).astype(BF16)
    o = 2 * A_W
    zq = jnp.dot(h, w_ref[:, o:o + ATT_W], preferred_element_type=F32)
    q_out[...] = (zq * (HEAD_DIM ** -0.5)).astype(BF16)
    o += ATT_W
    kv_out[...] = jnp.dot(h, w_ref[:, o:o + 2 * KV_W], preferred_element_type=F32).astype(BF16)
    o += 2 * KV_W
    zc = jnp.dot(h, w_ref[:, o:o + 3 * C_W], preferred_element_type=F32)
    cc_out[:, 0:C_W] = (zc[:, 2 * C_W:] * zc[:, :C_W]).astype(BF16)
    cc_out[:, C_W:] = zc[:, C_W:2 * C_W].astype(BF16)


def _pre(layer, fused, xs_in, mod4, norm_g, w_in_b, T, npt, tps):
    nt = T // TM
    seq = lambda i: _seq_of_tile(i, npt, tps)
    row = lambda i: (i, 0)
    mod_spec = lambda l: pl.BlockSpec((None, None, 1, 6 * D), lambda i: (l, seq(i), 0, 0))
    if fused:
        x_specs = [pl.BlockSpec((TM, D), row), pl.BlockSpec((TM, D), row), mod_spec(layer - 1)]
    else:
        nst = xs_in[1].shape[0] // TM
        x_specs = [pl.BlockSpec((TM, D), lambda i: (jnp.minimum(i, npt - 1), 0)),
                   pl.BlockSpec((TM, D), lambda i: (jnp.clip(i - npt, 0, nst - 1), 0))]
    in_specs = x_specs + [mod_spec(layer),
                          pl.BlockSpec((None, 1, D), lambda i: (layer, 0, 0)),
                          _resident((None, D, D_IN), lambda i: (layer, 0, 0))]
    out_shape = [jax.ShapeDtypeStruct((T, A_W), BF16), jax.ShapeDtypeStruct((T, ATT_W), BF16),
                 jax.ShapeDtypeStruct((T, 2 * KV_W), BF16), jax.ShapeDtypeStruct((T, 2 * C_W), BF16)]
    out_specs = [pl.BlockSpec((TM, A_W), row), pl.BlockSpec((TM, ATT_W), row),
                 pl.BlockSpec((TM, 2 * KV_W), row), pl.BlockSpec((TM, 2 * C_W), row)]
    if fused:
        out_shape = [jax.ShapeDtypeStruct((T, D), F32)] + out_shape
        out_specs = [pl.BlockSpec((TM, D), row)] + out_specs
    args = list(xs_in) + ([mod4] if fused else []) + [mod4, norm_g, w_in_b]
    return pl.pallas_call(
        functools.partial(_pre_body, fused, npt),
        grid=(nt,), in_specs=in_specs, out_specs=out_specs, out_shape=out_shape,
        compiler_params=_cparams(),
    )(*args)


def _route_rows(lt):
    lg = [lt[r:r + 1, :] for r in range(N_GROUPS)]
    mg = functools.reduce(jnp.maximum, lg)
    zg = functools.reduce(jnp.add, [jnp.exp(l - mg) for l in lg])
    gp = 1.0 / zg
    g = jnp.where(lg[0] == mg, 0.0, jnp.where(lg[1] == mg, 1.0, jnp.where(lg[2] == mg, 2.0, 3.0)))
    le = [lt[N_GROUPS + e:N_GROUPS + e + 1, :] for e in range(N_EXP)]
    sel = [jnp.where(g == 0.0, le[j], jnp.where(g == 1.0, le[EPG + j],
                                                jnp.where(g == 2.0, le[2 * EPG + j], le[3 * EPG + j])))
           for j in range(EPG)]
    ms = functools.reduce(jnp.maximum, sel)
    es = [jnp.exp(s - ms) for s in sel]
    zs = functools.reduce(jnp.add, es)
    p = [e / zs for e in es]
    v1 = functools.reduce(jnp.maximum, p)
    i1 = jnp.where(p[0] == v1, 0.0, jnp.where(p[1] == v1, 1.0, jnp.where(p[2] == v1, 2.0, 3.0)))
    p2 = [jnp.where(i1 == float(j), -1.0, p[j]) for j in range(EPG)]
    v2 = functools.reduce(jnp.maximum, p2)
    i2 = jnp.where(p2[0] == v2, 0.0, jnp.where(p2[1] == v2, 1.0, jnp.where(p2[2] == v2, 2.0, 3.0)))
    den = v1 + v2
    wt1 = gp * (v1 / den)
    wt2 = gp * (v2 / den)
    first = i1 < i2
    a = jnp.where(first, i1, i2)
    b = jnp.where(first, i2, i1)
    w_a = jnp.where(first, wt1, wt2)
    w_b = jnp.where(first, wt2, wt1)
    pair = jnp.where(a == 0.0, b - 1.0, jnp.where(a == 1.0, b + 1.0, 5.0))
    return g * float(N_PAIR) + pair, w_a, w_b


def _mix_body(layer0, npt, tp, ss, *refs):
    if layer0:
        xp_ref, xs_ref = refs[0], refs[1]
        refs = refs[2:]
    else:
        x_ref = refs[0]
        refs = refs[1:]
    (a_cur, a_prev, a_next, q_ref, kv_cur, kv_prev, kv_next, cc_cur, cc_prev, cc_next,
     mod_ref, gf_ref, caw_ref, cab_ref, lng_ref, lnb_ref, ccw_ref, wout_ref, wrt_ref, br_ref,
     bias_ref, sink_ref, tri_ref,
     xmid_out, h2e_out, route_out, cnt_out,
     abuf, cbuf, kvbuf, mixed, carry) = refs

    i = pl.program_id(0)
    t0 = i * TM
    in_p = t0 < tp
    p0 = jnp.where(in_p, t0, (t0 - tp) % ss)
    slen = jnp.where(in_p, tp, ss)
    is_first = p0 == 0
    is_last = p0 + TM == slen

    @pl.when(i == 0)
    def _():
        carry[...] = jnp.zeros_like(carry)

    abuf[0:HALO, :] = jnp.where(is_first, 0.0, a_prev[...].astype(F32))
    abuf[HALO:HALO + TM, :] = a_cur[...].astype(F32)
    abuf[HALO + TM:2 * HALO + TM, :] = jnp.where(is_last, 0.0, a_next[...].astype(F32))
    rc = 64
    for r in range(TM // rc):
        acc = jnp.zeros((rc, A_W), F32)
        for j in range(CONV_A):
            s = r * rc + HALO - (CONV_A - 1) // 2 + j
            acc = acc + caw_ref[j:j + 1, :] * abuf[s:s + rc, :]
        acc = acc + cab_ref[...]
        mu = jnp.mean(acc, axis=-1, keepdims=True)
        xc = acc - mu
        var = jnp.mean(xc * xc, axis=-1, keepdims=True)
        yn = xc * lax.rsqrt(var + EPS) * lng_ref[...] + lnb_ref[...]
        mixed[r * rc:(r + 1) * rc, 0:A_W] = (yn * jax.nn.sigmoid(yn)).astype(BF16)

    cbuf[0:HALO, :] = jnp.where(is_first, 0.0, cc_prev[:, 0:C_W].astype(F32))
    cbuf[HALO:HALO + TM, :] = cc_cur[:, 0:C_W].astype(F32)
    cbuf[HALO + TM:2 * HALO + TM, :] = jnp.where(is_last, 0.0, cc_next[:, 0:C_W].astype(F32))
    for r in range(TM // rc):
        acc = jnp.zeros((rc, C_W), F32)
        for j in range(CONV_C):
            s = r * rc + HALO - (CONV_C - 1) // 2 + j
            acc = acc + ccw_ref[j:j + 1, :] * cbuf[s:s + rc, :]
        gate = cc_cur[r * rc:(r + 1) * rc, C_W:2 * C_W].astype(F32)
        mixed[r * rc:(r + 1) * rc, A_W + ATT_W:] = (gate * acc).astype(BF16)

    kvbuf[0:BLK, :] = kv_prev[...]
    kvbuf[BLK:BLK + TM, :] = kv_cur[...]
    kvbuf[BLK + TM:2 * BLK + TM, :] = kv_next[...]
    col = lax.broadcasted_iota(jnp.int32, (1, 3 * BLK), 1)
    pen_first = jnp.where(jnp.logical_and(is_first, col < BLK), NEG, 0.0)
    pen_last = jnp.where(jnp.logical_and(is_last, col >= 2 * BLK), NEG, 0.0)
    nqb = TM // BLK
    for qb in range(nqb):
        for kh in range(N_KV):
            qs = jnp.concatenate(
                [q_ref[qb * BLK:(qb + 1) * BLK, (kh * GRP + g) * HEAD_DIM:(kh * GRP + g + 1) * HEAD_DIM]
                 for g in range(GRP)], axis=0)
            kb = kvbuf[qb * BLK:qb * BLK + 3 * BLK, kh * HEAD_DIM:(kh + 1) * HEAD_DIM]
            vb = kvbuf[qb * BLK:qb * BLK + 3 * BLK, KV_W + kh * HEAD_DIM:KV_W + (kh + 1) * HEAD_DIM]
            s = lax.dot_general(qs, kb, (((1,), (1,)), ((), ())), preferred_element_type=F32)
            s = s + bias_ref[kh]
            if qb == 0:
                s = s + pen_first
            if qb == nqb - 1:
                s = s + pen_last
            sk = sink_ref[kh]
            m = jnp.maximum(jnp.max(s, axis=-1, keepdims=True), sk)
            p = jnp.exp(s - m)
            den = jnp.sum(p, axis=-1, keepdims=True) + jnp.exp(sk - m)
            o = jnp.dot(p.astype(BF16), vb, preferred_element_type=F32) / den
            for g in range(GRP):
                c0 = A_W + (kh * GRP + g) * HEAD_DIM
                mixed[qb * BLK:(qb + 1) * BLK, c0:c0 + HEAD_DIM] = o[g * BLK:(g + 1) * BLK].astype(BF16)

    mix = jnp.dot(mixed[...], wout_ref[...], preferred_element_type=F32)
    if layer0:
        x = jnp.where(i < npt, xp_ref[...], xs_ref[...])
    else:
        x = x_ref[...]
    xmid = x + mod_ref[:, 2 * D:3 * D] * mix
    xmid_out[...] = xmid
    h2 = _rms_mod(xmid, gf_ref[...], mod_ref[:, 4 * D:5 * D], mod_ref[:, 3 * D:4 * D])
    h2e_out[:, 0:D] = h2

    lt = lax.dot_general(wrt_ref[...], h2.astype(BF16), (((1,), (1,)), ((), ())),
                         preferred_element_type=F32) + br_ref[...]
    bucket, w_a, w_b = _route_rows(lt)
    rows = lax.broadcasted_iota(jnp.int32, (RROWS, TM), 0).astype(F32)
    oh = jnp.where(rows == bucket, 1.0, 0.0)
    before = jnp.dot(oh.astype(BF16), tri_ref[...], preferred_element_type=F32)
    rank = jnp.sum(oh * (before + carry[:, 0:1]), axis=0, keepdims=True)
    carry[...] = carry[...] + jnp.sum(oh, axis=1, keepdims=True)
    cnt_out[...] = carry[...]
    zrow = jnp.zeros((ROUTE_ROWS - 4, TM), F32)
    route_out[...] = jnp.concatenate([bucket, rank, w_a, w_b, zrow], axis=0)
    pay = jnp.concatenate([w_a, w_b, jnp.zeros((PAYLOAD - 2, TM), F32)], axis=0)
    h2e_out[:, D:ROW_W] = pay.T


def _mix(layer, layer0, x_in, a, q, kv, cc, mod4, norm_ffn_g, caw, cab, lng, lnb, ccw, w_out_b,
         wrt, br, bias, sinkc, tri, T, npt, tps, tp, ss):
    nt = T // TM
    seq = lambda i: _seq_of_tile(i, npt, tps)
    row = lambda i: (i, 0)
    hp = TM // HALO
    bp = TM // BLK
    prev = lambda per: (lambda i: (jnp.maximum(i * per - 1, 0), 0))
    nxt = lambda per, n: (lambda i: (jnp.minimum((i + 1) * per, n - 1), 0))
    const2 = lambda i: (0, 0)
    lsel = lambda i: (layer, 0, 0)
    if layer0:
        nst = x_in[1].shape[0] // TM
        x_specs = [pl.BlockSpec((TM, D), lambda i: (jnp.minimum(i, npt - 1), 0)),
                   pl.BlockSpec((TM, D), lambda i: (jnp.clip(i - npt, 0, nst - 1), 0))]
    else:
        x_specs = [pl.BlockSpec((TM, D), row)]
    in_specs = x_specs + [
        pl.BlockSpec((TM, A_W), row),
        pl.BlockSpec((HALO, A_W), prev(hp)), pl.BlockSpec((HALO, A_W), nxt(hp, T // HALO)),
        pl.BlockSpec((TM, ATT_W), row),
        pl.BlockSpec((TM, 2 * KV_W), row),
        pl.BlockSpec((BLK, 2 * KV_W), prev(bp)), pl.BlockSpec((BLK, 2 * KV_W), nxt(bp, T // BLK)),
        pl.BlockSpec((TM, 2 * C_W), row),
        pl.BlockSpec((HALO, 2 * C_W), prev(hp)), pl.BlockSpec((HALO, 2 * C_W), nxt(hp, T // HALO)),
        pl.BlockSpec((None, None, 1, 6 * D), lambda i: (layer, seq(i), 0, 0)),
        pl.BlockSpec((None, 1, D), lsel),
        pl.BlockSpec((None, 32, A_W), lsel),
        pl.BlockSpec((None, 1, A_W), lsel), pl.BlockSpec((None, 1, A_W), lsel),
        pl.BlockSpec((None, 1, A_W), lsel),
        pl.BlockSpec((None, 8, C_W), lsel),
        _resident((None, D, D), lsel),
        pl.BlockSpec((None, RROWS, D), lsel),
        pl.BlockSpec((None, RROWS, 1), lsel),
        pl.BlockSpec((N_KV, GRP * BLK, 3 * BLK), lambda i: (0, 0, 0)),
        pl.BlockSpec((None, N_KV, GRP * BLK, 1), lambda i: (layer, 0, 0, 0)),
        pl.BlockSpec((TM, TM), const2),
    ]
    out_shape = [jax.ShapeDtypeStruct((T, D), F32), jax.ShapeDtypeStruct((T, ROW_W), F32),
                 jax.ShapeDtypeStruct((ROUTE_ROWS, T), F32), jax.ShapeDtypeStruct((RROWS, 128), F32)]
    out_specs = [pl.BlockSpec((TM, D), row), pl.BlockSpec((TM, ROW_W), row),
                 pl.BlockSpec((ROUTE_ROWS, TM), lambda i: (0, i)), pl.BlockSpec((RROWS, 128), const2)]
    scratch = [pltpu.VMEM((TM + 2 * HALO, A_W), F32), pltpu.VMEM((TM + 2 * HALO, C_W), F32),
               pltpu.VMEM((TM + 2 * BLK, 2 * KV_W), BF16), pltpu.VMEM((TM, D), BF16),
               pltpu.VMEM((RROWS, 128), F32)]
    args = list(x_in) + [a, a, a, q, kv, kv, kv, cc, cc, cc, mod4, norm_ffn_g, caw, cab, lng, lnb, ccw,
                         w_out_b, wrt, br, bias, sinkc, tri]
    return pl.pallas_call(
        functools.partial(_mix_body, layer0, npt, tp, ss),
        grid=(nt,), in_specs=in_specs, out_specs=out_specs, out_shape=out_shape,
        scratch_shapes=scratch, compiler_params=_cparams(),
    )(*args)


def _row_dma(src, dst, s, d, sem):
    return pltpu.make_async_copy(src.at[pl.ds(s, 1)], dst.at[pl.ds(d, 1)], sem)


def _move_rows_body(to_sorted, pos_ref, src_ref, dst_ref, sem):
    base = pl.program_id(0) * DMA_CHUNK

    def issue(r, c):
        t = base + r
        p = pos_ref[t]
        if to_sorted:
            _row_dma(src_ref, dst_ref, t, p, sem).start()
        else:
            _row_dma(src_ref, dst_ref, p, t, sem).start()
        return c

    lax.fori_loop(0, DMA_CHUNK, issue, 0)

    def drain(r, c):
        _row_dma(src_ref, dst_ref, 0, 0, sem).wait()
        return c

    lax.fori_loop(0, DMA_CHUNK, drain, 0)


def _move_rows(to_sorted, pos, src, n_out, T):
    width = src.shape[1]
    return pl.pallas_call(
        functools.partial(_move_rows_body, to_sorted),
        grid_spec=pltpu.PrefetchScalarGridSpec(
            num_scalar_prefetch=1, grid=(T // DMA_CHUNK,),
            in_specs=[pl.BlockSpec(memory_space=pl.ANY)],
            out_specs=pl.BlockSpec(memory_space=pl.ANY),
            scratch_shapes=[pltpu.SemaphoreType.DMA(())]),
        out_shape=jax.ShapeDtypeStruct((n_out, width), F32),
        compiler_params=pltpu.CompilerParams(dimension_semantics=("arbitrary",)),
    )(pos, src)


def _expert_half(x, wgu_ref, wd_ref):
    gu = jnp.dot(x, wgu_ref[...], preferred_element_type=F32)
    gate = gu[:, :D_EXP]
    he = gate * jax.nn.sigmoid(gate) * gu[:, D_EXP:]
    return jnp.dot(he.astype(BF16), wd_ref[...], preferred_element_type=F32)


def _expert_body(ea_ref, eb_ref, nv_ref, blk_ref, xs_ref, wgu_a, wd_a, wgu_b, wd_b, y_ref):
    j = pl.program_id(0)
    nv = nv_ref[j]

    @pl.when(nv > 0)
    def _():
        live = lax.broadcasted_iota(jnp.int32, (TME, 1), 0) < nv
        x = jnp.where(live, xs_ref[:, 0:D], 0.0).astype(BF16)
        w_a = jnp.where(live, xs_ref[:, D:D + 1], 0.0)
        w_b = jnp.where(live, xs_ref[:, D + 1:D + 2], 0.0)
        y_ref[...] = w_a * _expert_half(x, wgu_a, wd_a) + w_b * _expert_half(x, wgu_b, wd_b)

    @pl.when(nv == 0)
    def _():
        y_ref[...] = jnp.zeros_like(y_ref)


def _experts(layer, tile_ea, tile_eb, tile_nv, tile_blk, xs, wgu_b, wd_b, n_pad):
    nt = n_pad // TME
    wsel = lambda ref_idx: (lambda j, ea, eb, nv, blk: (layer, (ea, eb)[ref_idx][j], 0, 0))
    return pl.pallas_call(
        _expert_body,
        grid_spec=pltpu.PrefetchScalarGridSpec(
            num_scalar_prefetch=4, grid=(nt,),
            in_specs=[pl.BlockSpec((TME, ROW_W), lambda j, ea, eb, nv, blk: (blk[j], 0)),
                      pl.BlockSpec((None, None, D, 2 * D_EXP), wsel(0)),
                      pl.BlockSpec((None, None, D_EXP, D), wsel(0)),
                      pl.BlockSpec((None, None, D, 2 * D_EXP), wsel(1)),
                      pl.BlockSpec((None, None, D_EXP, D), wsel(1))],
            out_specs=pl.BlockSpec((TME, D), lambda j, ea, eb, nv, blk: (j, 0))),
        out_shape=jax.ShapeDtypeStruct((n_pad, D), F32),
        compiler_params=_cparams(),
    )(tile_ea, tile_eb, tile_nv, tile_blk, xs, wgu_b, wd_b, wgu_b, wd_b)


def _final_body(xm_ref, y_ref, mod_ref, g_ref, o_ref):
    x = xm_ref[...] + mod_ref[:, 5 * D:6 * D] * y_ref[...]
    o_ref[...] = x * lax.rsqrt(jnp.mean(x * x, axis=-1, keepdims=True) + EPS) * g_ref[...]


def _final(layer, xmid, y, mod4, final_g, tile0, ntiles, npt, tps):
    seq = lambda i: _seq_of_tile(i + tile0, npt, tps)
    off = lambda i: (i + tile0, 0)
    return pl.pallas_call(
        _final_body,
        grid=(ntiles,),
        in_specs=[pl.BlockSpec((TM, D), off), pl.BlockSpec((TM, D), off),
                  pl.BlockSpec((None, None, 1, 6 * D), lambda i: (layer, seq(i), 0, 0)),
                  pl.BlockSpec((1, D), lambda i: (0, 0))],
        out_specs=pl.BlockSpec((TM, D), lambda i: (i, 0)),
        out_shape=jax.ShapeDtypeStruct((ntiles * TM, D), F32),
        compiler_params=_cparams(),
    )(xmid, y, mod4, final_g)


def _bucket_tables():
    ea, eb = [], []
    for g in range(N_GROUPS):
        for a in range(EPG):
            for b in range(a + 1, EPG):
                ea.append(g * EPG + a)
                eb.append(g * EPG + b)
    return np.asarray(ea, np.int32), np.asarray(eb, np.int32)


def _plan(route, counts, n_pad):
    bucket = route[0].astype(jnp.int32)
    rank = route[1].astype(jnp.int32)
    cnt = counts[:N_BUCKET, 0].astype(jnp.int32)
    padded = (cnt + TME - 1) // TME * TME
    ends = jnp.cumsum(padded)
    starts = ends - padded
    pos = starts[bucket] + rank
    nt = n_pad // TME
    tile_start = jnp.arange(nt, dtype=jnp.int32) * TME
    n_used = ends[-1] // TME
    tb = jnp.sum(tile_start[:, None] >= ends[None, :], axis=1).astype(jnp.int32)
    last_b = jnp.max(jnp.where(cnt > 0, jnp.arange(N_BUCKET, dtype=jnp.int32), 0))
    used = jnp.arange(nt, dtype=jnp.int32) < n_used
    tb = jnp.where(used, jnp.minimum(tb, N_BUCKET - 1), last_b)
    ea_tab, eb_tab = _bucket_tables()
    tile_ea = jnp.asarray(ea_tab)[tb]
    tile_eb = jnp.asarray(eb_tab)[tb]
    tile_nv = jnp.where(used, jnp.clip(starts[tb] + cnt[tb] - tile_start, 0, TME), 0).astype(jnp.int32)
    tile_blk = jnp.minimum(jnp.arange(nt, dtype=jnp.int32), jnp.maximum(n_used - 1, 0))
    return pos, tile_ea, tile_eb, tile_nv, tile_blk


def _attn_bias():
    r = np.arange(BLK)[:, None]
    c = np.arange(3 * BLK)[None, :]
    dist = np.abs(c - BLK - r).astype(np.float32)
    slopes = 2.0 ** (-8.0 * np.arange(1, N_HEADS + 1, dtype=np.float32) / N_HEADS)
    tab = np.where(dist[None] <= WINDOW, -slopes[:, None, None] * dist[None], NEG).astype(np.float32)
    return jnp.asarray(tab.reshape(N_KV, GRP * BLK, 3 * BLK))


def kernel(x_prompt, x_sample, c_prompt, c_sample, norm_mix_g, norm_ffn_g, w_ada, b_ada, w_in, w_out,
           conv_a_w, conv_a_b, ln_a_g, ln_a_b, attn_sink, conv_c_w, w_router_group, b_router_group,
           w_router_expert, b_router_expert, w_gate, w_up, w_down, final_norm_g):
    depth = w_in.shape[0]
    bp, sp, _ = x_prompt.shape
    bs, ss, _ = x_sample.shape
    assert bp == 1 and sp % TM == 0 and ss % TM == 0 and (bp + bs) <= 8
    tp = bp * sp
    T = tp + bs * ss
    assert T % DMA_CHUNK == 0
    npt, tps = tp // TM, ss // TM
    n_pad = T + N_BUCKET * TME

    xp = x_prompt.reshape(tp, D)
    xs = x_sample.reshape(bs * ss, D)
    c_all = jnp.concatenate([c_prompt, c_sample, jnp.zeros((8 - bp - bs, D), F32)], axis=0)
    mod4 = _ada(c_all, w_ada, b_ada).reshape(depth, 8, 1, 6 * D)

    w_in_b = w_in.astype(BF16)
    w_out_b = w_out.astype(BF16)
    wgu_b = jnp.concatenate([w_gate, w_up], axis=-1).astype(BF16)
    wd_b = w_down.astype(BF16)
    wrt = jnp.concatenate([w_router_group, w_router_expert,
                           jnp.zeros((depth, D, RROWS - N_GROUPS - N_EXP), F32)], axis=-1)
    wrt = jnp.swapaxes(wrt, 1, 2).astype(BF16)
    br = jnp.concatenate([b_router_group, b_router_expert,
                          jnp.zeros((depth, RROWS - N_GROUPS - N_EXP), F32)], axis=-1)[..., None]
    caw = jnp.pad(conv_a_w, ((0, 0), (0, 32 - CONV_A), (0, 0)))
    ccw = jnp.pad(conv_c_w, ((0, 0), (0, 8 - CONV_C), (0, 0)))
    r3 = lambda v: v.reshape(depth, 1, -1)
    sinkc = jnp.broadcast_to(attn_sink.reshape(depth, N_KV, GRP, 1, 1),
                             (depth, N_KV, GRP, BLK, 1)).reshape(depth, N_KV, GRP * BLK, 1)
    bias = _attn_bias()
    tri = jnp.asarray(np.triu(np.ones((TM, TM), np.float32), 1)).astype(BF16)

    x_mid = y = None
    for l in range(depth):
        if l == 0:
            a, q, kv, cc = _pre(l, False, (xp, xs), mod4, r3(norm_mix_g), w_in_b, T, npt, tps)
            x_in = (xp, xs)
        else:
            x, a, q, kv, cc = _pre(l, True, (x_mid, y), mod4, r3(norm_mix_g), w_in_b, T, npt, tps)
            x_in = (x,)
        x_mid, h2e, route, counts = _mix(
            l, l == 0, x_in, a, q, kv, cc, mod4, r3(norm_ffn_g), caw, r3(conv_a_b), r3(ln_a_g),
            r3(ln_a_b), ccw, w_out_b, wrt, br, bias, sinkc, tri, T, npt, tps, tp, ss)
        pos, tile_ea, tile_eb, tile_nv, tile_blk = _plan(route, counts, n_pad)
        xsort = _move_rows(True, pos, h2e, n_pad, T)
        ysort = _experts(l, tile_ea, tile_eb, tile_nv, tile_blk, xsort, wgu_b, wd_b, n_pad)
        y = _move_rows(False, pos, ysort, T, T)

    fg = final_norm_g.reshape(1, D)
    y_p = _final(depth - 1, x_mid, y, mod4, fg, 0, npt, npt, tps)
    y_s = _final(depth - 1, x_mid, y, mod4, fg, npt, bs * tps, npt, tps)
    return y_p.reshape(bp, sp, D), y_s.reshape(bs, ss, D)
```

```python
import functools

import jax
import jax.numpy as jnp
import numpy as np
from jax import lax
from jax.experimental import pallas as pl
from jax.experimental.pallas import tpu as pltpu

F32 = jnp.float32
BF16 = jnp.bfloat16

D = 2048
N_HEADS = 8
HEAD_DIM = 128
N_KV = 2
GRP = N_HEADS // N_KV
A_W = 512
C_W = 512
ATT_W = N_HEADS * HEAD_DIM
KV_W = N_KV * HEAD_DIM
D_IN = 4096
CONV_A = 31
CONV_C = 3
WINDOW = 128
BLK = 128
N_GROUPS = 4
EPG = 4
N_EXP = N_GROUPS * EPG
D_EXP = 512
N_PAIR = 6
N_BUCKET = N_GROUPS * N_PAIR
EPS = 1e-6
NEG = -1e30
LOG2E = 1.4426950408889634

TM = 256
TME = 256
HALO = 16
LT = D // 128
LP = LT + 1
ROUTE_ROWS = 8
RROWS = 32
RLANES = 128
DMA_CHUNK = 2048
VMEM_LIMIT = 56 * 1024 * 1024


def _cparams(sem=("arbitrary",)):
    return pltpu.CompilerParams(dimension_semantics=sem, vmem_limit_bytes=VMEM_LIMIT)


def _resident(shape, index_map):
    return pl.BlockSpec(shape, index_map, pipeline_mode=pl.Buffered(1))


def _ada_body(c_ref, w_ref, b_ref, o_ref):
    c = c_ref[...]
    sc = c * jax.nn.sigmoid(c)
    o_ref[...] = jnp.dot(sc, w_ref[...], preferred_element_type=F32,
                         precision=lax.Precision.HIGHEST) + b_ref[...]


def _ada(c_all, w_ada, b_ada):
    depth = w_ada.shape[0]
    tn = 1024
    return pl.pallas_call(
        _ada_body,
        grid=(depth, 6 * D // tn),
        in_specs=[pl.BlockSpec((8, D), lambda l, j: (0, 0)),
                  pl.BlockSpec((None, D, tn), lambda l, j: (l, 0, j)),
                  pl.BlockSpec((None, 1, tn), lambda l, j: (l, 0, j))],
        out_specs=pl.BlockSpec((None, 8, tn), lambda l, j: (l, 0, j)),
        out_shape=jax.ShapeDtypeStruct((depth, 8, 6 * D), F32),
        compiler_params=_cparams(("arbitrary", "arbitrary")),
    )(c_all, w_ada, b_ada.reshape(depth, 1, 6 * D))


def _seq_of_tile(i, npt, tps):
    return jnp.where(i < npt, 0, 1 + (i - npt) // tps)


def _rows_to_tile(ref, n):
    return jnp.concatenate([ref[pl.ds(c, n, stride=LP), :] for c in range(LT)], axis=1)


def _tile_to_rows(ref, val, n):
    for c in range(LT):
        ref[pl.ds(c, n, stride=LP), :] = val[:, c * 128:(c + 1) * 128]
    ref[pl.ds(LT, n, stride=LP), :] = jnp.zeros((n, 128), F32)


def _rms_mod(x, g, scale, shift):
    y = x * lax.rsqrt(jnp.mean(x * x, axis=-1, keepdims=True) + EPS) * g
    return y * (1.0 + scale) + shift


def _pre_body(fused, npt, *refs):
    if fused:
        (xm_ref, y_ref, modp_ref, mod_ref, g_ref, w_ref,
         x_out, a_out, q_out, kv_out, cc_out) = refs
        x = xm_ref[...] + modp_ref[:, 5 * D:6 * D] * _rows_to_tile(y_ref, TM)
        x_out[...] = x
    else:
        xp_ref, xs_ref, mod_ref, g_ref, w_ref, a_out, q_out, kv_out, cc_out = refs
        x = jnp.where(pl.program_id(0) < npt, xp_ref[...], xs_ref[...])
    h = _rms_mod(x, g_ref[...], mod_ref[:, D:2 * D], mod_ref[:, 0:D]).astype(BF16)

    za = jnp.dot(h, w_ref[:, 0:2 * A_W], preferred_element_type=F32)
    a_out[...] = (za[:, :A_W] * jax.nn.sigmoid(za[:, A_W:])).astype(BF16)
    o = 2 * A_W
    zq = jnp.dot(h, w_ref[:, o:o + ATT_W], preferred_element_type=F32)
    q_out[...] = (zq * (LOG2E * HEAD_DIM ** -0.5)).astype(BF16)
    o += ATT_W
    kv_out[...] = jnp.dot(h, w_ref[:, o:o + 2 * KV_W], preferred_element_type=F32).astype(BF16)
    o += 2 * KV_W
    zc = jnp.dot(h, w_ref[:, o:o + 3 * C_W], preferred_element_type=F32)
    cc_out[:, 0:C_W] = (zc[:, 2 * C_W:] * zc[:, :C_W]).astype(BF16)
    cc_out[:, C_W:] = zc[:, C_W:2 * C_W].astype(BF16)


def _pre(layer, fused, xs_in, mod4, norm_g, w_in_b, T, npt, tps):
    nt = T // TM
    seq = lambda i: _seq_of_tile(i, npt, tps)
    row = lambda i: (i, 0)
    mod_spec = lambda l: pl.BlockSpec((None, None, 1, 6 * D), lambda i: (l, seq(i), 0, 0))
    if fused:
        x_specs = [pl.BlockSpec((TM, D), row), pl.BlockSpec((TM * LP, 128), row), mod_spec(layer - 1)]
    else:
        nst = xs_in[1].shape[0] // TM
        x_specs = [pl.BlockSpec((TM, D), lambda i: (jnp.minimum(i, npt - 1), 0)),
                   pl.BlockSpec((TM, D), lambda i: (jnp.clip(i - npt, 0, nst - 1), 0))]
    in_specs = x_specs + [mod_spec(layer),
                          pl.BlockSpec((None, 1, D), lambda i: (layer, 0, 0)),
                          _resident((None, D, D_IN), lambda i: (layer, 0, 0))]
    out_shape = [jax.ShapeDtypeStruct((T, A_W), BF16), jax.ShapeDtypeStruct((T, ATT_W), BF16),
                 jax.ShapeDtypeStruct((T, 2 * KV_W), BF16), jax.ShapeDtypeStruct((T, 2 * C_W), BF16)]
    out_specs = [pl.BlockSpec((TM, A_W), row), pl.BlockSpec((TM, ATT_W), row),
                 pl.BlockSpec((TM, 2 * KV_W), row), pl.BlockSpec((TM, 2 * C_W), row)]
    if fused:
        out_shape = [jax.ShapeDtypeStruct((T, D), F32)] + out_shape
        out_specs = [pl.BlockSpec((TM, D), row)] + out_specs
    args = list(xs_in) + ([mod4] if fused else []) + [mod4, norm_g, w_in_b]
    return pl.pallas_call(
        functools.partial(_pre_body, fused, npt),
        grid=(nt,), in_specs=in_specs, out_specs=out_specs, out_shape=out_shape,
        compiler_params=_cparams(),
    )(*args)


def _route_rows(lt):
    lg = [lt[r:r + 1, :] for r in range(N_GROUPS)]
    mg = functools.reduce(jnp.maximum, lg)
    g = jnp.where(lg[0] == mg, 0.0, jnp.where(lg[1] == mg, 1.0, jnp.where(lg[2] == mg, 2.0, 3.0)))
    le = [lt[N_GROUPS + e:N_GROUPS + e + 1, :] for e in range(N_EXP)]
    sel = [jnp.where(g == 0.0, le[j], jnp.where(g == 1.0, le[EPG + j],
                                                jnp.where(g == 2.0, le[2 * EPG + j], le[3 * EPG + j])))
           for j in range(EPG)]
    ms = functools.reduce(jnp.maximum, sel)
    es = [jnp.exp(s - ms) for s in sel]
    zs = functools.reduce(jnp.add, es)
    p = [e / zs for e in es]
    v1 = functools.reduce(jnp.maximum, p)
    i1 = jnp.where(p[0] == v1, 0.0, jnp.where(p[1] == v1, 1.0, jnp.where(p[2] == v1, 2.0, 3.0)))
    p2 = [jnp.where(i1 == float(j), -1.0, p[j]) for j in range(EPG)]
    v2 = functools.reduce(jnp.maximum, p2)
    i2 = jnp.where(p2[0] == v2, 0.0, jnp.where(p2[1] == v2, 1.0, jnp.where(p2[2] == v2, 2.0, 3.0)))
    a = jnp.minimum(i1, i2)
    b = jnp.maximum(i1, i2)
    pair = jnp.where(a == 0.0, b - 1.0, jnp.where(a == 1.0, b + 1.0, 5.0))
    return g * float(N_PAIR) + pair


def _mix_body(layer0, npt, tp, ss, *refs):
    if layer0:
        xp_ref, xs_ref = refs[0], refs[1]
        refs = refs[2:]
    else:
        x_ref = refs[0]
        refs = refs[1:]
    (a_cur, a_prev, a_next, q_ref, kv_cur, kv_prev, kv_next, cc_cur, cc_prev, cc_next,
     mod_ref, gf_ref, caw_ref, cab_ref, lng_ref, lnb_ref, ccw_ref, wout_ref, wrt_ref, br_ref,
     bias_ref, sink_ref, tri_ref,
     xmid_out, h2r_out, route_out, cnt_out,
     abuf, cbuf, kvbuf, mixed, carry) = refs

    i = pl.program_id(0)
    t0 = i * TM
    in_p = t0 < tp
    p0 = jnp.where(in_p, t0, (t0 - tp) % ss)
    slen = jnp.where(in_p, tp, ss)
    is_first = p0 == 0
    is_last = p0 + TM == slen

    @pl.when(i == 0)
    def _():
        carry[...] = jnp.zeros_like(carry)

    abuf[0, 0:HALO, :] = jnp.where(is_first, 0.0, a_prev[...].astype(F32))
    abuf[0, HALO:HALO + TM, :] = a_cur[...].astype(F32)
    abuf[0, HALO + TM:2 * HALO + TM, :] = jnp.where(is_last, 0.0, a_next[...].astype(F32))
    for k in range(1, 8):
        abuf[k, 0:TM + 24, :] = abuf[0, k:k + TM + 24, :]
    rc = 64
    for r in range(TM // rc):
        acc = jnp.zeros((rc, A_W), F32)
        for j in range(CONV_A):
            s = HALO - (CONV_A - 1) // 2 + j
            o = r * rc + s // 8 * 8
            acc = acc + caw_ref[j:j + 1, :] * abuf[s % 8, o:o + rc, :]
        acc = acc + cab_ref[...]
        mu = jnp.mean(acc, axis=-1, keepdims=True)
        xc = acc - mu
        var = jnp.mean(xc * xc, axis=-1, keepdims=True)
        yn = xc * lax.rsqrt(var + EPS) * lng_ref[...] + lnb_ref[...]
        mixed[r * rc:(r + 1) * rc, 0:A_W] = (yn * jax.nn.sigmoid(yn)).astype(BF16)

    cbuf[0:HALO, :] = jnp.where(is_first, 0.0, cc_prev[:, 0:C_W].astype(F32))
    cbuf[HALO:HALO + TM, :] = cc_cur[:, 0:C_W].astype(F32)
    cbuf[HALO + TM:2 * HALO + TM, :] = jnp.where(is_last, 0.0, cc_next[:, 0:C_W].astype(F32))
    for r in range(TM // rc):
        acc = jnp.zeros((rc, C_W), F32)
        for j in range(CONV_C):
            s = r * rc + HALO - (CONV_C - 1) // 2 + j
            acc = acc + ccw_ref[j:j + 1, :] * cbuf[s:s + rc, :]
        gate = cc_cur[r * rc:(r + 1) * rc, C_W:2 * C_W].astype(F32)
        mixed[r * rc:(r + 1) * rc, A_W + ATT_W:] = (gate * acc).astype(BF16)

    kvbuf[0:BLK, :] = kv_prev[...]
    kvbuf[BLK:BLK + TM, :] = kv_cur[...]
    kvbuf[BLK + TM:2 * BLK + TM, :] = kv_next[...]
    col = lax.broadcasted_iota(jnp.int32, (1, 3 * BLK), 1)
    pen_first = jnp.where(jnp.logical_and(is_first, col < BLK), NEG, 0.0)
    pen_last = jnp.where(jnp.logical_and(is_last, col >= 2 * BLK), NEG, 0.0)
    nqb = TM // BLK
    for qb in range(nqb):
        for kh in range(N_KV):
            qs = jnp.concatenate(
                [q_ref[qb * BLK:(qb + 1) * BLK, (kh * GRP + g) * HEAD_DIM:(kh * GRP + g + 1) * HEAD_DIM]
                 for g in range(GRP)], axis=0)
            kb = kvbuf[qb * BLK:qb * BLK + 3 * BLK, kh * HEAD_DIM:(kh + 1) * HEAD_DIM]
            vb = kvbuf[qb * BLK:qb * BLK + 3 * BLK, KV_W + kh * HEAD_DIM:KV_W + (kh + 1) * HEAD_DIM]
            s = lax.dot_general(qs, kb, (((1,), (1,)), ((), ())), preferred_element_type=F32)
            s = s + bias_ref[kh]
            if qb == 0:
                s = s + pen_first
            if qb == nqb - 1:
                s = s + pen_last
            sk = sink_ref[kh]
            m = jnp.maximum(jnp.max(s, axis=-1, keepdims=True), sk)
            p = jnp.exp2(s - m)
            den = jnp.sum(p, axis=-1, keepdims=True) + jnp.exp2(sk - m)
            o = jnp.dot(p.astype(BF16), vb, preferred_element_type=F32) / den
            for g in range(GRP):
                c0 = A_W + (kh * GRP + g) * HEAD_DIM
                mixed[qb * BLK:(qb + 1) * BLK, c0:c0 + HEAD_DIM] = o[g * BLK:(g + 1) * BLK].astype(BF16)

    mix = jnp.dot(mixed[...], wout_ref[...], preferred_element_type=F32)
    if layer0:
        x = jnp.where(i < npt, xp_ref[...], xs_ref[...])
    else:
        x = x_ref[...]
    xmid = x + mod_ref[:, 2 * D:3 * D] * mix
    xmid_out[...] = xmid
    h2 = _rms_mod(xmid, gf_ref[...], mod_ref[:, 4 * D:5 * D], mod_ref[:, 3 * D:4 * D])
    _tile_to_rows(h2r_out, h2, TM)

    lt = lax.dot_general(wrt_ref[...], h2.astype(BF16), (((1,), (1,)), ((), ())),
                         preferred_element_type=F32) + br_ref[...]
    bucket = _route_rows(lt)
    rows = lax.broadcasted_iota(jnp.int32, (RROWS, TM), 0).astype(F32)
    oh = jnp.where(rows == bucket, 1.0, 0.0)
    before = jnp.dot(oh.astype(BF16), tri_ref[...], preferred_element_type=F32)
    rank = jnp.sum(oh * (before + carry[:, 0:1]), axis=0, keepdims=True)
    carry[...] = carry[...] + jnp.sum(oh, axis=1, keepdims=True)
    cnt_out[...] = carry[...]
    zrow = jnp.zeros((ROUTE_ROWS - 2, TM), F32)
    route_out[...] = jnp.concatenate([bucket, rank, zrow], axis=0)


def _mix(layer, layer0, x_in, a, q, kv, cc, mod4, norm_ffn_g, caw, cab, lng, lnb, ccw, w_out_b,
         wrt, br, bias, sinkc, tri, T, npt, tps, tp, ss):
    nt = T // TM
    seq = lambda i: _seq_of_tile(i, npt, tps)
    row = lambda i: (i, 0)
    hp = TM // HALO
    bp = TM // BLK
    prev = lambda per: (lambda i: (jnp.maximum(i * per - 1, 0), 0))
    nxt = lambda per, n: (lambda i: (jnp.minimum((i + 1) * per, n - 1), 0))
    const2 = lambda i: (0, 0)
    lsel = lambda i: (layer, 0, 0)
    if layer0:
        nst = x_in[1].shape[0] // TM
        x_specs = [pl.BlockSpec((TM, D), lambda i: (jnp.minimum(i, npt - 1), 0)),
                   pl.BlockSpec((TM, D), lambda i: (jnp.clip(i - npt, 0, nst - 1), 0))]
    else:
        x_specs = [pl.BlockSpec((TM, D), row)]
    in_specs = x_specs + [
        pl.BlockSpec((TM, A_W), row),
        pl.BlockSpec((HALO, A_W), prev(hp)), pl.BlockSpec((HALO, A_W), nxt(hp, T // HALO)),
        pl.BlockSpec((TM, ATT_W), row),
        pl.BlockSpec((TM, 2 * KV_W), row),
        pl.BlockSpec((BLK, 2 * KV_W), prev(bp)), pl.BlockSpec((BLK, 2 * KV_W), nxt(bp, T // BLK)),
        pl.BlockSpec((TM, 2 * C_W), row),
        pl.BlockSpec((HALO, 2 * C_W), prev(hp)), pl.BlockSpec((HALO, 2 * C_W), nxt(hp, T // HALO)),
        pl.BlockSpec((None, None, 1, 6 * D), lambda i: (layer, seq(i), 0, 0)),
        pl.BlockSpec((None, 1, D), lsel),
        pl.BlockSpec((None, 32, A_W), lsel),
        pl.BlockSpec((None, 1, A_W), lsel), pl.BlockSpec((None, 1, A_W), lsel),
        pl.BlockSpec((None, 1, A_W), lsel),
        pl.BlockSpec((None, 8, C_W), lsel),
        _resident((None, D, D), lsel),
        pl.BlockSpec((None, RROWS, D), lsel),
        pl.BlockSpec((None, RROWS, 1), lsel),
        pl.BlockSpec((N_KV, GRP * BLK, 3 * BLK), lambda i: (0, 0, 0)),
        pl.BlockSpec((None, N_KV, GRP * BLK, 1), lambda i: (layer, 0, 0, 0)),
        pl.BlockSpec((TM, TM), const2),
    ]
    out_shape = [jax.ShapeDtypeStruct((T, D), F32), jax.ShapeDtypeStruct((T * LP, 128), F32),
                 jax.ShapeDtypeStruct((ROUTE_ROWS, T), F32), jax.ShapeDtypeStruct((RROWS, 128), F32)]
    out_specs = [pl.BlockSpec((TM, D), row), pl.BlockSpec((TM * LP, 128), row),
                 pl.BlockSpec((ROUTE_ROWS, TM), lambda i: (0, i)), pl.BlockSpec((RROWS, 128), const2)]
    scratch = [pltpu.VMEM((8, TM + 2 * HALO, A_W), F32), pltpu.VMEM((TM + 2 * HALO, C_W), F32),
               pltpu.VMEM((TM + 2 * BLK, 2 * KV_W), BF16), pltpu.VMEM((TM, D), BF16),
               pltpu.VMEM((RROWS, 128), F32)]
    args = list(x_in) + [a, a, a, q, kv, kv, kv, cc, cc, cc, mod4, norm_ffn_g, caw, cab, lng, lnb, ccw,
                         w_out_b, wrt, br, bias, sinkc, tri]
    return pl.pallas_call(
        functools.partial(_mix_body, layer0, npt, tp, ss),
        grid=(nt,), in_specs=in_specs, out_specs=out_specs, out_shape=out_shape,
        scratch_shapes=scratch, compiler_params=_cparams(),
    )(*args)


def _row_dma(src, dst, s, d, sem):
    return pltpu.make_async_copy(src.at[pl.ds(s * LP, LP)], dst.at[pl.ds(d * LP, LP)], sem)


def _move_rows_body(to_sorted, pos_ref, src_ref, dst_ref, sem):
    base = pl.program_id(0) * DMA_CHUNK

    def issue(r, c):
        t = base + r
        p = pos_ref[t]
        if to_sorted:
            _row_dma(src_ref, dst_ref, t, p, sem).start()
        else:
            _row_dma(src_ref, dst_ref, p, t, sem).start()
        return c

    lax.fori_loop(0, DMA_CHUNK, issue, 0, unroll=8)

    def drain(r, c):
        _row_dma(src_ref, dst_ref, 0, 0, sem).wait()
        return c

    lax.fori_loop(0, DMA_CHUNK, drain, 0, unroll=8)


def _move_rows(to_sorted, pos, src, n_out, T):
    return pl.pallas_call(
        functools.partial(_move_rows_body, to_sorted),
        grid_spec=pltpu.PrefetchScalarGridSpec(
            num_scalar_prefetch=1, grid=(T // DMA_CHUNK,),
            in_specs=[pl.BlockSpec(memory_space=pl.ANY)],
            out_specs=pl.BlockSpec(memory_space=pl.ANY),
            scratch_shapes=[pltpu.SemaphoreType.DMA(())]),
        out_shape=jax.ShapeDtypeStruct((n_out * LP, 128), F32),
        compiler_params=pltpu.CompilerParams(dimension_semantics=("arbitrary",)),
    )(pos, src)


def _expert_half(x, wgu_ref, wd_ref):
    gu = jnp.dot(x, wgu_ref[...], preferred_element_type=F32)
    gate = gu[:, :D_EXP]
    he = gate * jax.nn.sigmoid(gate) * gu[:, D_EXP:]
    return jnp.dot(he.astype(BF16), wd_ref[...], preferred_element_type=F32)


def _combine_weights(x, wr_ref, br_ref, g, a, b):
    lc = jnp.dot(x, wr_ref[...], preferred_element_type=F32) + br_ref[...]
    lane = lax.broadcasted_iota(jnp.int32, (1, RLANES), 1)
    rsum = lambda v: jnp.sum(v, axis=-1, keepdims=True)
    gmask = lane < N_GROUPS
    mg = jnp.max(jnp.where(gmask, lc, NEG), axis=-1, keepdims=True)
    zg = rsum(jnp.exp(jnp.where(gmask, lc - mg, NEG)))
    gp = jnp.exp(rsum(jnp.where(lane == g, lc, 0.0)) - mg) / zg
    e0 = N_GROUPS + EPG * g
    emask = jnp.logical_and(lane >= e0, lane < e0 + EPG)
    ms = jnp.max(jnp.where(emask, lc, NEG), axis=-1, keepdims=True)
    es = jnp.exp(jnp.where(emask, lc - ms, NEG))
    zs = rsum(es)
    pa = rsum(jnp.where(lane == e0 + a, es, 0.0)) / zs
    pb = rsum(jnp.where(lane == e0 + b, es, 0.0)) / zs
    den = pa + pb
    return gp * (pa / den), gp * (pb / den)


def _expert_body(ea_ref, eb_ref, nv_ref, blk_ref, xs_ref, wr_ref, br_ref, wgu_a, wd_a, wgu_b, wd_b, y_ref):
    j = pl.program_id(0)
    nv = nv_ref[j]

    @pl.when(nv > 0)
    def _():
        live = lax.broadcasted_iota(jnp.int32, (TME, 1), 0) < nv
        x = jnp.where(live, _rows_to_tile(xs_ref, TME), 0.0).astype(BF16)
        ea = ea_ref[j]
        w_a, w_b = _combine_weights(x, wr_ref, br_ref, ea // EPG, ea % EPG, eb_ref[j] % EPG)
        y = w_a * _expert_half(x, wgu_a, wd_a) + w_b * _expert_half(x, wgu_b, wd_b)
        _tile_to_rows(y_ref, y, TME)

    @pl.when(nv == 0)
    def _():
        y_ref[...] = jnp.zeros_like(y_ref)


def _experts(layer, tile_ea, tile_eb, tile_nv, tile_blk, xs, wr, brl, wgu_b, wd_b, n_pad):
    nt = n_pad // TME
    wsel = lambda ref_idx: (lambda j, ea, eb, nv, blk: (layer, (ea, eb)[ref_idx][j], 0, 0))
    lsel = lambda j, ea, eb, nv, blk: (layer, 0, 0)
    return pl.pallas_call(
        _expert_body,
        grid_spec=pltpu.PrefetchScalarGridSpec(
            num_scalar_prefetch=4, grid=(nt,),
            in_specs=[pl.BlockSpec((TME * LP, 128), lambda j, ea, eb, nv, blk: (blk[j], 0)),
                      pl.BlockSpec((None, D, RLANES), lsel),
                      pl.BlockSpec((None, 1, RLANES), lsel),
                      pl.BlockSpec((None, None, D, 2 * D_EXP), wsel(0)),
                      pl.BlockSpec((None, None, D_EXP, D), wsel(0)),
                      pl.BlockSpec((None, None, D, 2 * D_EXP), wsel(1)),
                      pl.BlockSpec((None, None, D_EXP, D), wsel(1))],
            out_specs=pl.BlockSpec((TME * LP, 128), lambda j, ea, eb, nv, blk: (j, 0))),
        out_shape=jax.ShapeDtypeStruct((n_pad * LP, 128), F32),
        compiler_params=_cparams(),
    )(tile_ea, tile_eb, tile_nv, tile_blk, xs, wr, brl, wgu_b, wd_b, wgu_b, wd_b)


def _final_body(xm_ref, y_ref, mod_ref, g_ref, o_ref):
    x = xm_ref[...] + mod_ref[:, 5 * D:6 * D] * _rows_to_tile(y_ref, TM)
    o_ref[...] = x * lax.rsqrt(jnp.mean(x * x, axis=-1, keepdims=True) + EPS) * g_ref[...]


def _final(layer, xmid, y, mod4, final_g, tile0, ntiles, npt, tps):
    seq = lambda i: _seq_of_tile(i + tile0, npt, tps)
    off = lambda i: (i + tile0, 0)
    return pl.pallas_call(
        _final_body,
        grid=(ntiles,),
        in_specs=[pl.BlockSpec((TM, D), off), pl.BlockSpec((TM * LP, 128), off),
                  pl.BlockSpec((None, None, 1, 6 * D), lambda i: (layer, seq(i), 0, 0)),
                  pl.BlockSpec((1, D), lambda i: (0, 0))],
        out_specs=pl.BlockSpec((TM, D), lambda i: (i, 0)),
        out_shape=jax.ShapeDtypeStruct((ntiles * TM, D), F32),
        compiler_params=_cparams(),
    )(xmid, y, mod4, final_g)


def _bucket_tables():
    ea, eb = [], []
    for g in range(N_GROUPS):
        for a in range(EPG):
            for b in range(a + 1, EPG):
                ea.append(g * EPG + a)
                eb.append(g * EPG + b)
    return np.asarray(ea, np.int32), np.asarray(eb, np.int32)


def _plan(route, counts, n_pad):
    bucket = route[0].astype(jnp.int32)
    rank = route[1].astype(jnp.int32)
    cnt = counts[:N_BUCKET, 0].astype(jnp.int32)
    padded = (cnt + TME - 1) // TME * TME
    ends = jnp.cumsum(padded)
    starts = ends - padded
    pos = starts[bucket] + rank
    nt = n_pad // TME
    tile_start = jnp.arange(nt, dtype=jnp.int32) * TME
    n_used = ends[-1] // TME
    tb = jnp.sum(tile_start[:, None] >= ends[None, :], axis=1).astype(jnp.int32)
    last_b = jnp.max(jnp.where(cnt > 0, jnp.arange(N_BUCKET, dtype=jnp.int32), 0))
    used = jnp.arange(nt, dtype=jnp.int32) < n_used
    tb = jnp.where(used, jnp.minimum(tb, N_BUCKET - 1), last_b)
    ea_tab, eb_tab = _bucket_tables()
    tile_ea = jnp.asarray(ea_tab)[tb]
    tile_eb = jnp.asarray(eb_tab)[tb]
    tile_nv = jnp.where(used, jnp.clip(starts[tb] + cnt[tb] - tile_start, 0, TME), 0).astype(jnp.int32)
    tile_blk = jnp.minimum(jnp.arange(nt, dtype=jnp.int32), jnp.maximum(n_used - 1, 0))
    return pos, tile_ea, tile_eb, tile_nv, tile_blk


def _attn_bias():
    r = np.arange(BLK)[:, None]
    c = np.arange(3 * BLK)[None, :]
    dist = np.abs(c - BLK - r).astype(np.float32)
    slopes = 2.0 ** (-8.0 * np.arange(1, N_HEADS + 1, dtype=np.float32) / N_HEADS)
    tab = np.where(dist[None] <= WINDOW, -LOG2E * slopes[:, None, None] * dist[None], NEG).astype(np.float32)
    return jnp.asarray(tab.reshape(N_KV, GRP * BLK, 3 * BLK))


def kernel(x_prompt, x_sample, c_prompt, c_sample, norm_mix_g, norm_ffn_g, w_ada, b_ada, w_in, w_out,
           conv_a_w, conv_a_b, ln_a_g, ln_a_b, attn_sink, conv_c_w, w_router_group, b_router_group,
           w_router_expert, b_router_expert, w_gate, w_up, w_down, final_norm_g):
    depth = w_in.shape[0]
    bp, sp, _ = x_prompt.shape
    bs, ss, _ = x_sample.shape
    assert bp == 1 and sp % TM == 0 and ss % TM == 0 and (bp + bs) <= 8
    tp = bp * sp
    T = tp + bs * ss
    assert T % DMA_CHUNK == 0
    npt, tps = tp // TM, ss // TM
    n_pad = T + N_BUCKET * TME

    xp = x_prompt.reshape(tp, D)
    xs = x_sample.reshape(bs * ss, D)
    c_all = jnp.concatenate([c_prompt, c_sample, jnp.zeros((8 - bp - bs, D), F32)], axis=0)
    mod4 = _ada(c_all, w_ada, b_ada).reshape(depth, 8, 1, 6 * D)

    w_in_b = w_in.astype(BF16)
    w_out_b = w_out.astype(BF16)
    wgu_b = jnp.concatenate([w_gate, w_up], axis=-1).astype(BF16)
    wd_b = w_down.astype(BF16)
    wrt = jnp.concatenate([w_router_group, w_router_expert,
                           jnp.zeros((depth, D, RROWS - N_GROUPS - N_EXP), F32)], axis=-1)
    wr_cols = jnp.pad(wrt, ((0, 0), (0, 0), (0, RLANES - RROWS))).astype(BF16)
    wrt = jnp.swapaxes(wrt, 1, 2).astype(BF16)
    br = jnp.concatenate([b_router_group, b_router_expert,
                          jnp.zeros((depth, RROWS - N_GROUPS - N_EXP), F32)], axis=-1)
    br_lanes = jnp.pad(br, ((0, 0), (0, RLANES - RROWS)))[:, None, :]
    br = br[..., None]
    caw = jnp.pad(conv_a_w, ((0, 0), (0, 32 - CONV_A), (0, 0)))
    ccw = jnp.pad(conv_c_w, ((0, 0), (0, 8 - CONV_C), (0, 0)))
    r3 = lambda v: v.reshape(depth, 1, -1)
    sinkc = jnp.broadcast_to((LOG2E * attn_sink).reshape(depth, N_KV, GRP, 1, 1),
                             (depth, N_KV, GRP, BLK, 1)).reshape(depth, N_KV, GRP * BLK, 1)
    bias = _attn_bias()
    tri = jnp.asarray(np.triu(np.ones((TM, TM), np.float32), 1)).astype(BF16)

    x_mid = y = None
    for l in range(depth):
        if l == 0:
            a, q, kv, cc = _pre(l, False, (xp, xs), mod4, r3(norm_mix_g), w_in_b, T, npt, tps)
            x_in = (xp, xs)
        else:
            x, a, q, kv, cc = _pre(l, True, (x_mid, y), mod4, r3(norm_mix_g), w_in_b, T, npt, tps)
            x_in = (x,)
        x_mid, h2r, route, counts = _mix(
            l, l == 0, x_in, a, q, kv, cc, mod4, r3(norm_ffn_g), caw, r3(conv_a_b), r3(ln_a_g),
            r3(ln_a_b), ccw, w_out_b, wrt, br, bias, sinkc, tri, T, npt, tps, tp, ss)
        pos, tile_ea, tile_eb, tile_nv, tile_blk = _plan(route, counts, n_pad)
        xsort = _move_rows(True, pos, h2r, n_pad, T)
        ysort = _experts(l, tile_ea, tile_eb, tile_nv, tile_blk, xsort, wr_cols, br_lanes, wgu_b, wd_b, n_pad)
        y = _move_rows(False, pos, ysort, T, T)

    fg = final_norm_g.reshape(1, D)
    y_p = _final(depth - 1, x_mid, y, mod4, fg, 0, npt, npt, tps)
    y_s = _final(depth - 1, x_mid, y, mod4, fg, npt, bs * tps, npt, tps)
    return y_p.reshape(bp, sp, D), y_s.reshape(bs, ss, D)
```

```python
import functools

import jax
import jax.numpy as jnp
import numpy as np
from jax import lax
from jax.experimental import pallas as pl
from jax.experimental.pallas import tpu as pltpu

F32 = jnp.float32
BF16 = jnp.bfloat16

D = 2048
N_HEADS = 8
HEAD_DIM = 128
N_KV = 2
GRP = N_HEADS // N_KV
A_W = 512
C_W = 512
ATT_W = N_HEADS * HEAD_DIM
KV_W = N_KV * HEAD_DIM
D_IN = 4096
CONV_A = 31
CONV_C = 3
WINDOW = 128
BLK = 128
N_GROUPS = 4
EPG = 4
N_EXP = N_GROUPS * EPG
D_EXP = 512
N_PAIR = 6
N_BUCKET = N_GROUPS * N_PAIR
EPS = 1e-6
NEG = -1e30
LOG2E = 1.4426950408889634

TM = 256
TME = 256
HALO = 16
LT = D // 128
LP = LT + 1
ROUTE_ROWS = 8
RROWS = 32
RLANES = 128
DMA_CHUNK = 2048
VMEM_LIMIT = 56 * 1024 * 1024


def _cparams(sem=("arbitrary",)):
    return pltpu.CompilerParams(dimension_semantics=sem, vmem_limit_bytes=VMEM_LIMIT)


def _resident(shape, index_map):
    return pl.BlockSpec(shape, index_map, pipeline_mode=pl.Buffered(1))


def _ada_body(c_ref, w_ref, b_ref, o_ref):
    c = c_ref[...]
    sc = c * jax.nn.sigmoid(c)
    o_ref[...] = jnp.dot(sc, w_ref[...], preferred_element_type=F32,
                         precision=lax.Precision.HIGHEST) + b_ref[...]


def _ada(c_all, w_ada, b_ada):
    depth = w_ada.shape[0]
    tn = 1024
    return pl.pallas_call(
        _ada_body,
        grid=(depth, 6 * D // tn),
        in_specs=[pl.BlockSpec((8, D), lambda l, j: (0, 0)),
                  pl.BlockSpec((None, D, tn), lambda l, j: (l, 0, j)),
                  pl.BlockSpec((None, 1, tn), lambda l, j: (l, 0, j))],
        out_specs=pl.BlockSpec((None, 8, tn), lambda l, j: (l, 0, j)),
        out_shape=jax.ShapeDtypeStruct((depth, 8, 6 * D), F32),
        compiler_params=_cparams(("arbitrary", "arbitrary")),
    )(c_all, w_ada, b_ada.reshape(depth, 1, 6 * D))


def _seq_of_tile(i, npt, tps):
    return jnp.where(i < npt, 0, 1 + (i - npt) // tps)


def _rows_to_tile(ref, n):
    return jnp.concatenate([ref[pl.ds(c, n, stride=LP), :] for c in range(LT)], axis=1)


def _tile_to_rows(ref, val, n):
    for c in range(LT):
        ref[pl.ds(c, n, stride=LP), :] = val[:, c * 128:(c + 1) * 128]
    ref[pl.ds(LT, n, stride=LP), :] = jnp.zeros((n, 128), F32)


def _rms_mod(x, g, scale, shift):
    y = x * lax.rsqrt(jnp.mean(x * x, axis=-1, keepdims=True) + EPS) * g
    return y * (1.0 + scale) + shift


def _pre_body(fused, npt, *refs):
    if fused:
        (xm_ref, y_ref, modp_ref, mod_ref, g_ref, w_ref,
         x_out, a_out, q_out, kv_out, cc_out) = refs
        x = xm_ref[...] + modp_ref[:, 5 * D:6 * D] * _rows_to_tile(y_ref, TM)
        x_out[...] = x
    else:
        xp_ref, xs_ref, mod_ref, g_ref, w_ref, a_out, q_out, kv_out, cc_out = refs
        x = jnp.where(pl.program_id(0) < npt, xp_ref[...], xs_ref[...])
    h = _rms_mod(x, g_ref[...], mod_ref[:, D:2 * D], mod_ref[:, 0:D]).astype(BF16)

    za = jnp.dot(h, w_ref[:, 0:2 * A_W], preferred_element_type=F32)
    a_out[...] = (za[:, :A_W] * jax.nn.sigmoid(za[:, A_W:])).astype(BF16)
    o = 2 * A_W
    zq = jnp.dot(h, w_ref[:, o:o + ATT_W], preferred_element_type=F32)
    q_out[...] = (zq * (LOG2E * HEAD_DIM ** -0.5)).astype(BF16)
    o += ATT_W
    kv_out[...] = jnp.dot(h, w_ref[:, o:o + 2 * KV_W], preferred_element_type=F32).astype(BF16)
    o += 2 * KV_W
    zc = jnp.dot(h, w_ref[:, o:o + 3 * C_W], preferred_element_type=F32)
    cc_out[:, 0:C_W] = (zc[:, 2 * C_W:] * zc[:, :C_W]).astype(BF16)
    cc_out[:, C_W:] = zc[:, C_W:2 * C_W].astype(BF16)


def _pre(layer, fused, xs_in, mod4, norm_g, w_in_b, T, npt, tps):
    nt = T // TM
    seq = lambda i: _seq_of_tile(i, npt, tps)
    row = lambda i: (i, 0)
    mod_spec = lambda l: pl.BlockSpec((None, None, 1, 6 * D), lambda i: (l, seq(i), 0, 0))
    if fused:
        x_specs = [pl.BlockSpec((TM, D), row), pl.BlockSpec((TM * LP, 128), row), mod_spec(layer - 1)]
    else:
        nst = xs_in[1].shape[0] // TM
        x_specs = [pl.BlockSpec((TM, D), lambda i: (jnp.minimum(i, npt - 1), 0)),
                   pl.BlockSpec((TM, D), lambda i: (jnp.clip(i - npt, 0, nst - 1), 0))]
    in_specs = x_specs + [mod_spec(layer),
                          pl.BlockSpec((None, 1, D), lambda i: (layer, 0, 0)),
                          _resident((None, D, D_IN), lambda i: (layer, 0, 0))]
    out_shape = [jax.ShapeDtypeStruct((T, A_W), BF16), jax.ShapeDtypeStruct((T, ATT_W), BF16),
                 jax.ShapeDtypeStruct((T, 2 * KV_W), BF16), jax.ShapeDtypeStruct((T, 2 * C_W), BF16)]
    out_specs = [pl.BlockSpec((TM, A_W), row), pl.BlockSpec((TM, ATT_W), row),
                 pl.BlockSpec((TM, 2 * KV_W), row), pl.BlockSpec((TM, 2 * C_W), row)]
    if fused:
        out_shape = [jax.ShapeDtypeStruct((T, D), F32)] + out_shape
        out_specs = [pl.BlockSpec((TM, D), row)] + out_specs
    args = list(xs_in) + ([mod4] if fused else []) + [mod4, norm_g, w_in_b]
    return pl.pallas_call(
        functools.partial(_pre_body, fused, npt),
        grid=(nt,), in_specs=in_specs, out_specs=out_specs, out_shape=out_shape,
        compiler_params=_cparams(),
    )(*args)


def _route_rows(lt):
    lg = [lt[r:r + 1, :] for r in range(N_GROUPS)]
    mg = functools.reduce(jnp.maximum, lg)
    g = jnp.where(lg[0] == mg, 0.0, jnp.where(lg[1] == mg, 1.0, jnp.where(lg[2] == mg, 2.0, 3.0)))
    le = [lt[N_GROUPS + e:N_GROUPS + e + 1, :] for e in range(N_EXP)]
    sel = [jnp.where(g == 0.0, le[j], jnp.where(g == 1.0, le[EPG + j],
                                                jnp.where(g == 2.0, le[2 * EPG + j], le[3 * EPG + j])))
           for j in range(EPG)]
    ms = functools.reduce(jnp.maximum, sel)
    es = [jnp.exp(s - ms) for s in sel]
    zs = functools.reduce(jnp.add, es)
    p = [e / zs for e in es]
    v1 = functools.reduce(jnp.maximum, p)
    i1 = jnp.where(p[0] == v1, 0.0, jnp.where(p[1] == v1, 1.0, jnp.where(p[2] == v1, 2.0, 3.0)))
    p2 = [jnp.where(i1 == float(j), -1.0, p[j]) for j in range(EPG)]
    v2 = functools.reduce(jnp.maximum, p2)
    i2 = jnp.where(p2[0] == v2, 0.0, jnp.where(p2[1] == v2, 1.0, jnp.where(p2[2] == v2, 2.0, 3.0)))
    a = jnp.minimum(i1, i2)
    b = jnp.maximum(i1, i2)
    pair = jnp.where(a == 0.0, b - 1.0, jnp.where(a == 1.0, b + 1.0, 5.0))
    return g * float(N_PAIR) + pair


def _mix_body(layer0, npt, tp, ss, *refs):
    if layer0:
        xp_ref, xs_ref = refs[0], refs[1]
        refs = refs[2:]
    else:
        x_ref = refs[0]
        refs = refs[1:]
    (a_cur, a_prev, a_next, q_ref, kv_cur, kv_prev, kv_next, cc_cur, cc_prev, cc_next,
     mod_ref, gf_ref, caw_ref, cab_ref, lng_ref, lnb_ref, ccw_ref, wout_ref, wrt_ref, br_ref,
     bias_ref, sink_ref, tri_ref,
     xmid_out, h2r_out, route_out, cnt_out,
     abuf, cbuf, kvbuf, mixed, carry) = refs

    i = pl.program_id(0)
    t0 = i * TM
    in_p = t0 < tp
    p0 = jnp.where(in_p, t0, (t0 - tp) % ss)
    slen = jnp.where(in_p, tp, ss)
    is_first = p0 == 0
    is_last = p0 + TM == slen

    @pl.when(i == 0)
    def _():
        carry[...] = jnp.zeros_like(carry)

    abuf[0, 0:HALO, :] = jnp.where(is_first, 0.0, a_prev[...].astype(F32))
    abuf[0, HALO:HALO + TM, :] = a_cur[...].astype(F32)
    abuf[0, HALO + TM:2 * HALO + TM, :] = jnp.where(is_last, 0.0, a_next[...].astype(F32))
    for k in range(1, 8):
        abuf[k, 0:TM + 24, :] = abuf[0, k:k + TM + 24, :]
    rc = 64
    for r in range(TM // rc):
        acc = jnp.zeros((rc, A_W), F32)
        for j in range(CONV_A):
            s = HALO - (CONV_A - 1) // 2 + j
            o = r * rc + s // 8 * 8
            acc = acc + caw_ref[j:j + 1, :] * abuf[s % 8, o:o + rc, :]
        acc = acc + cab_ref[...]
        mu = jnp.mean(acc, axis=-1, keepdims=True)
        xc = acc - mu
        var = jnp.mean(xc * xc, axis=-1, keepdims=True)
        yn = xc * lax.rsqrt(var + EPS) * lng_ref[...] + lnb_ref[...]
        mixed[r * rc:(r + 1) * rc, 0:A_W] = (yn * jax.nn.sigmoid(yn)).astype(BF16)

    cbuf[0:HALO, :] = jnp.where(is_first, 0.0, cc_prev[:, 0:C_W].astype(F32))
    cbuf[HALO:HALO + TM, :] = cc_cur[:, 0:C_W].astype(F32)
    cbuf[HALO + TM:2 * HALO + TM, :] = jnp.where(is_last, 0.0, cc_next[:, 0:C_W].astype(F32))
    for r in range(TM // rc):
        acc = jnp.zeros((rc, C_W), F32)
        for j in range(CONV_C):
            s = r * rc + HALO - (CONV_C - 1) // 2 + j
            acc = acc + ccw_ref[j:j + 1, :] * cbuf[s:s + rc, :]
        gate = cc_cur[r * rc:(r + 1) * rc, C_W:2 * C_W].astype(F32)
        mixed[r * rc:(r + 1) * rc, A_W + ATT_W:] = (gate * acc).astype(BF16)

    kvbuf[0:BLK, :] = kv_prev[...]
    kvbuf[BLK:BLK + TM, :] = kv_cur[...]
    kvbuf[BLK + TM:2 * BLK + TM, :] = kv_next[...]
    col = lax.broadcasted_iota(jnp.int32, (1, 3 * BLK), 1)
    pen_first = jnp.where(jnp.logical_and(is_first, col < BLK), NEG, 0.0)
    pen_last = jnp.where(jnp.logical_and(is_last, col >= 2 * BLK), NEG, 0.0)
    nqb = TM // BLK
    for qb in range(nqb):
        for kh in range(N_KV):
            qs = jnp.concatenate(
                [q_ref[qb * BLK:(qb + 1) * BLK, (kh * GRP + g) * HEAD_DIM:(kh * GRP + g + 1) * HEAD_DIM]
                 for g in range(GRP)], axis=0)
            kb = kvbuf[qb * BLK:qb * BLK + 3 * BLK, kh * HEAD_DIM:(kh + 1) * HEAD_DIM]
            vb = kvbuf[qb * BLK:qb * BLK + 3 * BLK, KV_W + kh * HEAD_DIM:KV_W + (kh + 1) * HEAD_DIM]
            s = lax.dot_general(qs, kb, (((1,), (1,)), ((), ())), preferred_element_type=F32)
            s = s + bias_ref[kh]
            if qb == 0:
                s = s + pen_first
            if qb == nqb - 1:
                s = s + pen_last
            sk = sink_ref[kh]
            m = jnp.maximum(jnp.max(s, axis=-1, keepdims=True), sk)
            p = jnp.exp2(s - m)
            den = jnp.sum(p, axis=-1, keepdims=True) + jnp.exp2(sk - m)
            o = jnp.dot(p.astype(BF16), vb, preferred_element_type=F32) / den
            for g in range(GRP):
                c0 = A_W + (kh * GRP + g) * HEAD_DIM
                mixed[qb * BLK:(qb + 1) * BLK, c0:c0 + HEAD_DIM] = o[g * BLK:(g + 1) * BLK].astype(BF16)

    mix = jnp.dot(mixed[...], wout_ref[...], preferred_element_type=F32)
    if layer0:
        x = jnp.where(i < npt, xp_ref[...], xs_ref[...])
    else:
        x = x_ref[...]
    xmid = x + mod_ref[:, 2 * D:3 * D] * mix
    xmid_out[...] = xmid
    h2 = _rms_mod(xmid, gf_ref[...], mod_ref[:, 4 * D:5 * D], mod_ref[:, 3 * D:4 * D])
    _tile_to_rows(h2r_out, h2, TM)

    lt = lax.dot_general(wrt_ref[...], h2.astype(BF16), (((1,), (1,)), ((), ())),
                         preferred_element_type=F32) + br_ref[...]
    bucket = _route_rows(lt)
    rows = lax.broadcasted_iota(jnp.int32, (RROWS, TM), 0).astype(F32)
    oh = jnp.where(rows == bucket, 1.0, 0.0)
    before = jnp.dot(oh.astype(BF16), tri_ref[...], preferred_element_type=F32)
    rank = jnp.sum(oh * (before + carry[:, 0:1]), axis=0, keepdims=True)
    carry[...] = carry[...] + jnp.sum(oh, axis=1, keepdims=True)
    cnt_out[...] = carry[...]
    zrow = jnp.zeros((ROUTE_ROWS - 2, TM), F32)
    route_out[...] = jnp.concatenate([bucket, rank, zrow], axis=0)


def _mix(layer, layer0, x_in, a, q, kv, cc, mod4, norm_ffn_g, caw, cab, lng, lnb, ccw, w_out_b,
         wrt, br, bias, sinkc, tri, T, npt, tps, tp, ss):
    nt = T // TM
    seq = lambda i: _seq_of_tile(i, npt, tps)
    row = lambda i: (i, 0)
    hp = TM // HALO
    bp = TM // BLK
    prev = lambda per: (lambda i: (jnp.maximum(i * per - 1, 0), 0))
    nxt = lambda per, n: (lambda i: (jnp.minimum((i + 1) * per, n - 1), 0))
    const2 = lambda i: (0, 0)
    lsel = lambda i: (layer, 0, 0)
    if layer0:
        nst = x_in[1].shape[0] // TM
        x_specs = [pl.BlockSpec((TM, D), lambda i: (jnp.minimum(i, npt - 1), 0)),
                   pl.BlockSpec((TM, D), lambda i: (jnp.clip(i - npt, 0, nst - 1), 0))]
    else:
        x_specs = [pl.BlockSpec((TM, D), row)]
    in_specs = x_specs + [
        pl.BlockSpec((TM, A_W), row),
        pl.BlockSpec((HALO, A_W), prev(hp)), pl.BlockSpec((HALO, A_W), nxt(hp, T // HALO)),
        pl.BlockSpec((TM, ATT_W), row),
        pl.BlockSpec((TM, 2 * KV_W), row),
        pl.BlockSpec((BLK, 2 * KV_W), prev(bp)), pl.BlockSpec((BLK, 2 * KV_W), nxt(bp, T // BLK)),
        pl.BlockSpec((TM, 2 * C_W), row),
        pl.BlockSpec((HALO, 2 * C_W), prev(hp)), pl.BlockSpec((HALO, 2 * C_W), nxt(hp, T // HALO)),
        pl.BlockSpec((None, None, 1, 6 * D), lambda i: (layer, seq(i), 0, 0)),
        pl.BlockSpec((None, 1, D), lsel),
        pl.BlockSpec((None, 32, A_W), lsel),
        pl.BlockSpec((None, 1, A_W), lsel), pl.BlockSpec((None, 1, A_W), lsel),
        pl.BlockSpec((None, 1, A_W), lsel),
        pl.BlockSpec((None, 8, C_W), lsel),
        _resident((None, D, D), lsel),
        pl.BlockSpec((None, RROWS, D), lsel),
        pl.BlockSpec((None, RROWS, 1), lsel),
        pl.BlockSpec((N_KV, GRP * BLK, 3 * BLK), lambda i: (0, 0, 0)),
        pl.BlockSpec((None, N_KV, GRP * BLK, 1), lambda i: (layer, 0, 0, 0)),
        pl.BlockSpec((TM, TM), const2),
    ]
    out_shape = [jax.ShapeDtypeStruct((T, D), F32), jax.ShapeDtypeStruct((T * LP, 128), F32),
                 jax.ShapeDtypeStruct((ROUTE_ROWS, T), F32), jax.ShapeDtypeStruct((RROWS, 128), F32)]
    out_specs = [pl.BlockSpec((TM, D), row), pl.BlockSpec((TM * LP, 128), row),
                 pl.BlockSpec((ROUTE_ROWS, TM), lambda i: (0, i)), pl.BlockSpec((RROWS, 128), const2)]
    scratch = [pltpu.VMEM((8, TM + 2 * HALO, A_W), F32), pltpu.VMEM((TM + 2 * HALO, C_W), F32),
               pltpu.VMEM((TM + 2 * BLK, 2 * KV_W), BF16), pltpu.VMEM((TM, D), BF16),
               pltpu.VMEM((RROWS, 128), F32)]
    args = list(x_in) + [a, a, a, q, kv, kv, kv, cc, cc, cc, mod4, norm_ffn_g, caw, cab, lng, lnb, ccw,
                         w_out_b, wrt, br, bias, sinkc, tri]
    return pl.pallas_call(
        functools.partial(_mix_body, layer0, npt, tp, ss),
        grid=(nt,), in_specs=in_specs, out_specs=out_specs, out_shape=out_shape,
        scratch_shapes=scratch, compiler_params=_cparams(),
    )(*args)


def _row_dma(src, dst, s, d, sem):
    return pltpu.make_async_copy(src.at[pl.ds(s * LP, LP)], dst.at[pl.ds(d * LP, LP)], sem)


def _move_rows_body(to_sorted, pos_ref, src_ref, dst_ref, sem):
    base = pl.program_id(0) * TM

    def issue(r, c):
        p = pos_ref[base + r]
        if to_sorted:
            _row_dma(src_ref, dst_ref, r, p, sem).start()
        else:
            _row_dma(src_ref, dst_ref, p, r, sem).start()
        return c

    lax.fori_loop(0, TM, issue, 0, unroll=8)

    def drain(r, c):
        _row_dma(src_ref, dst_ref, 0, 0, sem).wait()
        return c

    lax.fori_loop(0, TM, drain, 0, unroll=8)


def _move_rows(to_sorted, pos, src, n_out, T):
    tile = pl.BlockSpec((TM * LP, 128), lambda i, pos: (i, 0))
    hbm = pl.BlockSpec(memory_space=pl.ANY)
    return pl.pallas_call(
        functools.partial(_move_rows_body, to_sorted),
        grid_spec=pltpu.PrefetchScalarGridSpec(
            num_scalar_prefetch=1, grid=(T // TM,),
            in_specs=[tile if to_sorted else hbm],
            out_specs=hbm if to_sorted else tile,
            scratch_shapes=[pltpu.SemaphoreType.DMA(())]),
        out_shape=jax.ShapeDtypeStruct((n_out * LP, 128), F32),
        compiler_params=_cparams(),
    )(pos, src)


def _expert_half(x, wgu_ref, wd_ref):
    gu = jnp.dot(x, wgu_ref[...], preferred_element_type=F32)
    gate = gu[:, :D_EXP]
    he = gate * jax.nn.sigmoid(gate) * gu[:, D_EXP:]
    return jnp.dot(he.astype(BF16), wd_ref[...], preferred_element_type=F32)


def _combine_weights(x, wr_ref, br_ref, g, a, b):
    lc = jnp.dot(x, wr_ref[...], preferred_element_type=F32) + br_ref[...]
    lane = lax.broadcasted_iota(jnp.int32, (1, RLANES), 1)
    rsum = lambda v: jnp.sum(v, axis=-1, keepdims=True)
    gmask = lane < N_GROUPS
    mg = jnp.max(jnp.where(gmask, lc, NEG), axis=-1, keepdims=True)
    zg = rsum(jnp.exp(jnp.where(gmask, lc - mg, NEG)))
    gp = jnp.exp(rsum(jnp.where(lane == g, lc, 0.0)) - mg) / zg
    e0 = N_GROUPS + EPG * g
    emask = jnp.logical_and(lane >= e0, lane < e0 + EPG)
    ms = jnp.max(jnp.where(emask, lc, NEG), axis=-1, keepdims=True)
    es = jnp.exp(jnp.where(emask, lc - ms, NEG))
    zs = rsum(es)
    pa = rsum(jnp.where(lane == e0 + a, es, 0.0)) / zs
    pb = rsum(jnp.where(lane == e0 + b, es, 0.0)) / zs
    den = pa + pb
    return gp * (pa / den), gp * (pb / den)


def _expert_body(ea_ref, eb_ref, nv_ref, blk_ref, xs_ref, wr_ref, br_ref, wgu_a, wd_a, wgu_b, wd_b, y_ref):
    j = pl.program_id(0)
    nv = nv_ref[j]

    @pl.when(nv > 0)
    def _():
        live = lax.broadcasted_iota(jnp.int32, (TME, 1), 0) < nv
        x = jnp.where(live, _rows_to_tile(xs_ref, TME), 0.0).astype(BF16)
        ea = ea_ref[j]
        w_a, w_b = _combine_weights(x, wr_ref, br_ref, ea // EPG, ea % EPG, eb_ref[j] % EPG)
        y = w_a * _expert_half(x, wgu_a, wd_a) + w_b * _expert_half(x, wgu_b, wd_b)
        _tile_to_rows(y_ref, y, TME)

    @pl.when(nv == 0)
    def _():
        y_ref[...] = jnp.zeros_like(y_ref)


def _experts(layer, tile_ea, tile_eb, tile_nv, tile_blk, xs, wr, brl, wgu_b, wd_b, n_pad):
    nt = n_pad // TME
    wsel = lambda ref_idx: (lambda j, ea, eb, nv, blk: (layer, (ea, eb)[ref_idx][j], 0, 0))
    lsel = lambda j, ea, eb, nv, blk: (layer, 0, 0)
    return pl.pallas_call(
        _expert_body,
        grid_spec=pltpu.PrefetchScalarGridSpec(
            num_scalar_prefetch=4, grid=(nt,),
            in_specs=[pl.BlockSpec((TME * LP, 128), lambda j, ea, eb, nv, blk: (blk[j], 0)),
                      pl.BlockSpec((None, D, RLANES), lsel),
                      pl.BlockSpec((None, 1, RLANES), lsel),
                      pl.BlockSpec((None, None, D, 2 * D_EXP), wsel(0)),
                      pl.BlockSpec((None, None, D_EXP, D), wsel(0)),
                      pl.BlockSpec((None, None, D, 2 * D_EXP), wsel(1)),
                      pl.BlockSpec((None, None, D_EXP, D), wsel(1))],
            out_specs=pl.BlockSpec((TME * LP, 128), lambda j, ea, eb, nv, blk: (j, 0))),
        out_shape=jax.ShapeDtypeStruct((n_pad * LP, 128), F32),
        compiler_params=_cparams(),
    )(tile_ea, tile_eb, tile_nv, tile_blk, xs, wr, brl, wgu_b, wd_b, wgu_b, wd_b)


def _final_body(xm_ref, y_ref, mod_ref, g_ref, o_ref):
    x = xm_ref[...] + mod_ref[:, 5 * D:6 * D] * _rows_to_tile(y_ref, TM)
    o_ref[...] = x * lax.rsqrt(jnp.mean(x * x, axis=-1, keepdims=True) + EPS) * g_ref[...]


def _final(layer, xmid, y, mod4, final_g, tile0, ntiles, npt, tps):
    seq = lambda i: _seq_of_tile(i + tile0, npt, tps)
    off = lambda i: (i + tile0, 0)
    return pl.pallas_call(
        _final_body,
        grid=(ntiles,),
        in_specs=[pl.BlockSpec((TM, D), off), pl.BlockSpec((TM * LP, 128), off),
                  pl.BlockSpec((None, None, 1, 6 * D), lambda i: (layer, seq(i), 0, 0)),
                  pl.BlockSpec((1, D), lambda i: (0, 0))],
        out_specs=pl.BlockSpec((TM, D), lambda i: (i, 0)),
        out_shape=jax.ShapeDtypeStruct((ntiles * TM, D), F32),
        compiler_params=_cparams(),
    )(xmid, y, mod4, final_g)


def _bucket_tables():
    ea, eb = [], []
    for g in range(N_GROUPS):
        for a in range(EPG):
            for b in range(a + 1, EPG):
                ea.append(g * EPG + a)
                eb.append(g * EPG + b)
    return np.asarray(ea, np.int32), np.asarray(eb, np.int32)


def _plan(route, counts, n_pad):
    bucket = route[0].astype(jnp.int32)
    rank = route[1].astype(jnp.int32)
    cnt = counts[:N_BUCKET, 0].astype(jnp.int32)
    padded = (cnt + TME - 1) // TME * TME
    ends = jnp.cumsum(padded)
    starts = ends - padded
    pos = starts[bucket] + rank
    nt = n_pad // TME
    tile_start = jnp.arange(nt, dtype=jnp.int32) * TME
    n_used = ends[-1] // TME
    tb = jnp.sum(tile_start[:, None] >= ends[None, :], axis=1).astype(jnp.int32)
    last_b = jnp.max(jnp.where(cnt > 0, jnp.arange(N_BUCKET, dtype=jnp.int32), 0))
    used = jnp.arange(nt, dtype=jnp.int32) < n_used
    tb = jnp.where(used, jnp.minimum(tb, N_BUCKET - 1), last_b)
    ea_tab, eb_tab = _bucket_tables()
    tile_ea = jnp.asarray(ea_tab)[tb]
    tile_eb = jnp.asarray(eb_tab)[tb]
    tile_nv = jnp.where(used, jnp.clip(starts[tb] + cnt[tb] - tile_start, 0, TME), 0).astype(jnp.int32)
    tile_blk = jnp.minimum(jnp.arange(nt, dtype=jnp.int32), jnp.maximum(n_used - 1, 0))
    return pos, tile_ea, tile_eb, tile_nv, tile_blk


def _attn_bias():
    r = np.arange(BLK)[:, None]
    c = np.arange(3 * BLK)[None, :]
    dist = np.abs(c - BLK - r).astype(np.float32)
    slopes = 2.0 ** (-8.0 * np.arange(1, N_HEADS + 1, dtype=np.float32) / N_HEADS)
    tab = np.where(dist[None] <= WINDOW, -LOG2E * slopes[:, None, None] * dist[None], NEG).astype(np.float32)
    return jnp.asarray(tab.reshape(N_KV, GRP * BLK, 3 * BLK))


def kernel(x_prompt, x_sample, c_prompt, c_sample, norm_mix_g, norm_ffn_g, w_ada, b_ada, w_in, w_out,
           conv_a_w, conv_a_b, ln_a_g, ln_a_b, attn_sink, conv_c_w, w_router_group, b_router_group,
           w_router_expert, b_router_expert, w_gate, w_up, w_down, final_norm_g):
    depth = w_in.shape[0]
    bp, sp, _ = x_prompt.shape
    bs, ss, _ = x_sample.shape
    assert bp == 1 and sp % TM == 0 and ss % TM == 0 and (bp + bs) <= 8
    tp = bp * sp
    T = tp + bs * ss
    assert T % DMA_CHUNK == 0
    npt, tps = tp // TM, ss // TM
    n_pad = T + N_BUCKET * TME

    xp = x_prompt.reshape(tp, D)
    xs = x_sample.reshape(bs * ss, D)
    c_all = jnp.concatenate([c_prompt, c_sample, jnp.zeros((8 - bp - bs, D), F32)], axis=0)
    mod4 = _ada(c_all, w_ada, b_ada).reshape(depth, 8, 1, 6 * D)

    w_in_b = w_in.astype(BF16)
    w_out_b = w_out.astype(BF16)
    wgu_b = jnp.concatenate([w_gate, w_up], axis=-1).astype(BF16)
    wd_b = w_down.astype(BF16)
    wrt = jnp.concatenate([w_router_group, w_router_expert,
                           jnp.zeros((depth, D, RROWS - N_GROUPS - N_EXP), F32)], axis=-1)
    wr_cols = jnp.pad(wrt, ((0, 0), (0, 0), (0, RLANES - RROWS))).astype(BF16)
    wrt = jnp.swapaxes(wrt, 1, 2).astype(BF16)
    br = jnp.concatenate([b_router_group, b_router_expert,
                          jnp.zeros((depth, RROWS - N_GROUPS - N_EXP), F32)], axis=-1)
    br_lanes = jnp.pad(br, ((0, 0), (0, RLANES - RROWS)))[:, None, :]
    br = br[..., None]
    caw = jnp.pad(conv_a_w, ((0, 0), (0, 32 - CONV_A), (0, 0)))
    ccw = jnp.pad(conv_c_w, ((0, 0), (0, 8 - CONV_C), (0, 0)))
    r3 = lambda v: v.reshape(depth, 1, -1)
    sinkc = jnp.broadcast_to((LOG2E * attn_sink).reshape(depth, N_KV, GRP, 1, 1),
                             (depth, N_KV, GRP, BLK, 1)).reshape(depth, N_KV, GRP * BLK, 1)
    bias = _attn_bias()
    tri = jnp.asarray(np.triu(np.ones((TM, TM), np.float32), 1)).astype(BF16)

    x_mid = y = None
    for l in range(depth):
        if l == 0:
            a, q, kv, cc = _pre(l, False, (xp, xs), mod4, r3(norm_mix_g), w_in_b, T, npt, tps)
            x_in = (xp, xs)
        else:
            x, a, q, kv, cc = _pre(l, True, (x_mid, y), mod4, r3(norm_mix_g), w_in_b, T, npt, tps)
            x_in = (x,)
        x_mid, h2r, route, counts = _mix(
            l, l == 0, x_in, a, q, kv, cc, mod4, r3(norm_ffn_g), caw, r3(conv_a_b), r3(ln_a_g),
            r3(ln_a_b), ccw, w_out_b, wrt, br, bias, sinkc, tri, T, npt, tps, tp, ss)
        pos, tile_ea, tile_eb, tile_nv, tile_blk = _plan(route, counts, n_pad)
        xsort = _move_rows(True, pos, h2r, n_pad, T)
        ysort = _experts(l, tile_ea, tile_eb, tile_nv, tile_blk, xsort, wr_cols, br_lanes, wgu_b, wd_b, n_pad)
        y = _move_rows(False, pos, ysort, T, T)

    fg = final_norm_g.reshape(1, D)
    y_p = _final(depth - 1, x_mid, y, mod4, fg, 0, npt, npt, tps)
    y_s = _final(depth - 1, x_mid, y, mod4, fg, npt, bs * tps, npt, tps)
    return y_p.reshape(bp, sp, D), y_s.reshape(bs, ss, D)
```

```python
import functools

import jax
import jax.numpy as jnp
import numpy as np
from jax import lax
from jax.experimental import pallas as pl
from jax.experimental.pallas import tpu as pltpu

F32 = jnp.float32
BF16 = jnp.bfloat16

D = 2048
N_HEADS = 8
HEAD_DIM = 128
N_KV = 2
GRP = N_HEADS // N_KV
A_W = 512
C_W = 512
ATT_W = N_HEADS * HEAD_DIM
KV_W = N_KV * HEAD_DIM
D_IN = 4096
CONV_A = 31
CONV_C = 3
WINDOW = 128
BLK = 128
N_GROUPS = 4
EPG = 4
N_EXP = N_GROUPS * EPG
D_EXP = 512
N_PAIR = 6
N_BUCKET = N_GROUPS * N_PAIR
EPS = 1e-6
NEG = -1e30
LOG2E = 1.4426950408889634

TM = 256
TME = 256
HALO = 16
LT = D // 128
LP = LT + 1
ROUTE_ROWS = 8
RROWS = 32
RLANES = 128
VMEM_LIMIT = 56 * 1024 * 1024


def _cparams(sem=("arbitrary",)):
    return pltpu.CompilerParams(dimension_semantics=sem, vmem_limit_bytes=VMEM_LIMIT)


def _resident(shape, index_map):
    return pl.BlockSpec(shape, index_map, pipeline_mode=pl.Buffered(1))


def _ada_body(c_ref, w_ref, b_ref, o_ref):
    c = c_ref[...]
    sc = c * jax.nn.sigmoid(c)
    o_ref[...] = jnp.dot(sc, w_ref[...], preferred_element_type=F32,
                         precision=lax.Precision.HIGHEST) + b_ref[...]


def _ada(c_all, w_ada, b_ada):
    depth = w_ada.shape[0]
    tn = 1024
    return pl.pallas_call(
        _ada_body,
        grid=(depth, 6 * D // tn),
        in_specs=[pl.BlockSpec((8, D), lambda l, j: (0, 0)),
                  pl.BlockSpec((None, D, tn), lambda l, j: (l, 0, j)),
                  pl.BlockSpec((None, 1, tn), lambda l, j: (l, 0, j))],
        out_specs=pl.BlockSpec((None, 8, tn), lambda l, j: (l, 0, j)),
        out_shape=jax.ShapeDtypeStruct((depth, 8, 6 * D), F32),
        compiler_params=_cparams(("arbitrary", "arbitrary")),
    )(c_all, w_ada, b_ada.reshape(depth, 1, 6 * D))


def _seq_of_tile(i, npt, tps):
    return jnp.where(i < npt, 0, 1 + (i - npt) // tps)


def _rows_to_tile(ref, n):
    return jnp.concatenate([ref[pl.ds(c, n, stride=LP), :] for c in range(LT)], axis=1)


def _tile_to_rows(ref, val, n):
    for c in range(LT):
        ref[pl.ds(c, n, stride=LP), :] = val[:, c * 128:(c + 1) * 128]
    ref[pl.ds(LT, n, stride=LP), :] = jnp.zeros((n, 128), F32)


def _row_dma(src, dst, s, d, sem):
    return pltpu.make_async_copy(src.at[pl.ds(s * LP, LP)], dst.at[pl.ds(d * LP, LP)], sem)


def _gather_tile(idx_ref, src_hbm, buf, sem, n, live, base=0):
    i = pl.program_id(0)
    slot = i % 2

    def start(step, s):
        @pl.when(live(step))
        def _():
            def issue(r, c):
                _row_dma(src_hbm, buf.at[s], idx_ref[base + step * n + r], r, sem.at[s]).start()
                return c
            lax.fori_loop(0, n, issue, 0, unroll=8)

    @pl.when(i == 0)
    def _():
        start(i, slot)

    @pl.when(i + 1 < pl.num_programs(0))
    def _():
        start(i + 1, 1 - slot)

    @pl.when(live(i))
    def _():
        def drain(r, c):
            _row_dma(src_hbm, buf.at[slot], 0, 0, sem.at[slot]).wait()
            return c
        lax.fori_loop(0, n, drain, 0, unroll=8)

    return slot


def _rms_mod(x, g, scale, shift):
    y = x * lax.rsqrt(jnp.mean(x * x, axis=-1, keepdims=True) + EPS) * g
    return y * (1.0 + scale) + shift


def _pre_body(fused, npt, *refs):
    if fused:
        (pos_ref, xm_ref, ys_ref, modp_ref, mod_ref, g_ref, w_ref,
         x_out, a_out, q_out, kv_out, cc_out, ybuf, ysem) = refs
        slot = _gather_tile(pos_ref, ys_ref, ybuf, ysem, TM, lambda step: step >= 0)
        x = xm_ref[...] + modp_ref[:, 5 * D:6 * D] * _rows_to_tile(ybuf.at[slot], TM)
        x_out[...] = x
    else:
        xp_ref, xs_ref, mod_ref, g_ref, w_ref, a_out, q_out, kv_out, cc_out = refs
        x = jnp.where(pl.program_id(0) < npt, xp_ref[...], xs_ref[...])
    h = _rms_mod(x, g_ref[...], mod_ref[:, D:2 * D], mod_ref[:, 0:D]).astype(BF16)

    za = jnp.dot(h, w_ref[:, 0:2 * A_W], preferred_element_type=F32)
    a_out[...] = (za[:, :A_W] * jax.nn.sigmoid(za[:, A_W:])).astype(BF16)
    o = 2 * A_W
    zq = jnp.dot(h, w_ref[:, o:o + ATT_W], preferred_element_type=F32)
    q_out[...] = (zq * (LOG2E * HEAD_DIM ** -0.5)).astype(BF16)
    o += ATT_W
    kv_out[...] = jnp.dot(h, w_ref[:, o:o + 2 * KV_W], preferred_element_type=F32).astype(BF16)
    o += 2 * KV_W
    zc = jnp.dot(h, w_ref[:, o:o + 3 * C_W], preferred_element_type=F32)
    cc_out[:, 0:C_W] = (zc[:, 2 * C_W:] * zc[:, :C_W]).astype(BF16)
    cc_out[:, C_W:] = zc[:, C_W:2 * C_W].astype(BF16)


def _ring_scratch(n):
    return [pltpu.VMEM((2, n * LP, 128), F32), pltpu.SemaphoreType.DMA((2,))]


def _pre(layer, fused, xs_in, mod4, norm_g, w_in_b, T, npt, tps, pos=None):
    nt = T // TM
    seq = lambda i: _seq_of_tile(i, npt, tps)
    row = lambda i, *_: (i, 0)
    mod_spec = lambda l: pl.BlockSpec((None, None, 1, 6 * D), lambda i, *_: (l, seq(i), 0, 0))
    if fused:
        x_specs = [pl.BlockSpec((TM, D), row), pl.BlockSpec(memory_space=pl.ANY), mod_spec(layer - 1)]
    else:
        nst = xs_in[1].shape[0] // TM
        x_specs = [pl.BlockSpec((TM, D), lambda i: (jnp.minimum(i, npt - 1), 0)),
                   pl.BlockSpec((TM, D), lambda i: (jnp.clip(i - npt, 0, nst - 1), 0))]
    in_specs = x_specs + [mod_spec(layer),
                          pl.BlockSpec((None, 1, D), lambda i, *_: (layer, 0, 0)),
                          _resident((None, D, D_IN), lambda i, *_: (layer, 0, 0))]
    out_shape = [jax.ShapeDtypeStruct((T, A_W), BF16), jax.ShapeDtypeStruct((T, ATT_W), BF16),
                 jax.ShapeDtypeStruct((T, 2 * KV_W), BF16), jax.ShapeDtypeStruct((T, 2 * C_W), BF16)]
    out_specs = [pl.BlockSpec((TM, A_W), row), pl.BlockSpec((TM, ATT_W), row),
                 pl.BlockSpec((TM, 2 * KV_W), row), pl.BlockSpec((TM, 2 * C_W), row)]
    if fused:
        out_shape = [jax.ShapeDtypeStruct((T, D), F32)] + out_shape
        out_specs = [pl.BlockSpec((TM, D), row)] + out_specs
    args = ([pos] if fused else []) + list(xs_in) + ([mod4] if fused else []) + [mod4, norm_g, w_in_b]
    return pl.pallas_call(
        functools.partial(_pre_body, fused, npt),
        grid_spec=pltpu.PrefetchScalarGridSpec(
            num_scalar_prefetch=1 if fused else 0, grid=(nt,), in_specs=in_specs, out_specs=out_specs,
            scratch_shapes=_ring_scratch(TM) if fused else []),
        out_shape=out_shape, compiler_params=_cparams(),
    )(*args)


def _route_rows(lt):
    lg = [lt[r:r + 1, :] for r in range(N_GROUPS)]
    mg = functools.reduce(jnp.maximum, lg)
    g = jnp.where(lg[0] == mg, 0.0, jnp.where(lg[1] == mg, 1.0, jnp.where(lg[2] == mg, 2.0, 3.0)))
    le = [lt[N_GROUPS + e:N_GROUPS + e + 1, :] for e in range(N_EXP)]
    sel = [jnp.where(g == 0.0, le[j], jnp.where(g == 1.0, le[EPG + j],
                                                jnp.where(g == 2.0, le[2 * EPG + j], le[3 * EPG + j])))
           for j in range(EPG)]
    ms = functools.reduce(jnp.maximum, sel)
    es = [jnp.exp(s - ms) for s in sel]
    zs = functools.reduce(jnp.add, es)
    p = [e / zs for e in es]
    v1 = functools.reduce(jnp.maximum, p)
    i1 = jnp.where(p[0] == v1, 0.0, jnp.where(p[1] == v1, 1.0, jnp.where(p[2] == v1, 2.0, 3.0)))
    p2 = [jnp.where(i1 == float(j), -1.0, p[j]) for j in range(EPG)]
    v2 = functools.reduce(jnp.maximum, p2)
    i2 = jnp.where(p2[0] == v2, 0.0, jnp.where(p2[1] == v2, 1.0, jnp.where(p2[2] == v2, 2.0, 3.0)))
    a = jnp.minimum(i1, i2)
    b = jnp.maximum(i1, i2)
    pair = jnp.where(a == 0.0, b - 1.0, jnp.where(a == 1.0, b + 1.0, 5.0))
    return g * float(N_PAIR) + pair


def _mix_body(layer0, npt, tp, ss, *refs):
    if layer0:
        xp_ref, xs_ref = refs[0], refs[1]
        refs = refs[2:]
    else:
        x_ref = refs[0]
        refs = refs[1:]
    (a_cur, a_prev, a_next, q_ref, kv_cur, kv_prev, kv_next, cc_cur, cc_prev, cc_next,
     mod_ref, gf_ref, caw_ref, cab_ref, lng_ref, lnb_ref, ccw_ref, wout_ref, wrt_ref, br_ref,
     bias_ref, sink_ref, tri_ref,
     xmid_out, h2r_out, route_out, cnt_out,
     abuf, cbuf, kvbuf, mixed, carry) = refs

    i = pl.program_id(0)
    t0 = i * TM
    in_p = t0 < tp
    p0 = jnp.where(in_p, t0, (t0 - tp) % ss)
    slen = jnp.where(in_p, tp, ss)
    is_first = p0 == 0
    is_last = p0 + TM == slen

    @pl.when(i == 0)
    def _():
        carry[...] = jnp.zeros_like(carry)

    abuf[0, 0:HALO, :] = jnp.where(is_first, 0.0, a_prev[...].astype(F32))
    abuf[0, HALO:HALO + TM, :] = a_cur[...].astype(F32)
    abuf[0, HALO + TM:2 * HALO + TM, :] = jnp.where(is_last, 0.0, a_next[...].astype(F32))
    for k in range(1, 8):
        abuf[k, 0:TM + 24, :] = abuf[0, k:k + TM + 24, :]
    rc = 64
    for r in range(TM // rc):
        acc = jnp.zeros((rc, A_W), F32)
        for j in range(CONV_A):
            s = HALO - (CONV_A - 1) // 2 + j
            o = r * rc + s // 8 * 8
            acc = acc + caw_ref[j:j + 1, :] * abuf[s % 8, o:o + rc, :]
        acc = acc + cab_ref[...]
        mu = jnp.mean(acc, axis=-1, keepdims=True)
        xc = acc - mu
        var = jnp.mean(xc * xc, axis=-1, keepdims=True)
        yn = xc * lax.rsqrt(var + EPS) * lng_ref[...] + lnb_ref[...]
        mixed[r * rc:(r + 1) * rc, 0:A_W] = (yn * jax.nn.sigmoid(yn)).astype(BF16)

    cbuf[0:HALO, :] = jnp.where(is_first, 0.0, cc_prev[:, 0:C_W].astype(F32))
    cbuf[HALO:HALO + TM, :] = cc_cur[:, 0:C_W].astype(F32)
    cbuf[HALO + TM:2 * HALO + TM, :] = jnp.where(is_last, 0.0, cc_next[:, 0:C_W].astype(F32))
    for r in range(TM // rc):
        acc = jnp.zeros((rc, C_W), F32)
        for j in range(CONV_C):
            s = r * rc + HALO - (CONV_C - 1) // 2 + j
            acc = acc + ccw_ref[j:j + 1, :] * cbuf[s:s + rc, :]
        gate = cc_cur[r * rc:(r + 1) * rc, C_W:2 * C_W].astype(F32)
        mixed[r * rc:(r + 1) * rc, A_W + ATT_W:] = (gate * acc).astype(BF16)

    kvbuf[0:BLK, :] = kv_prev[...]
    kvbuf[BLK:BLK + TM, :] = kv_cur[...]
    kvbuf[BLK + TM:2 * BLK + TM, :] = kv_next[...]
    col = lax.broadcasted_iota(jnp.int32, (1, 3 * BLK), 1)
    pen_first = jnp.where(jnp.logical_and(is_first, col < BLK), NEG, 0.0)
    pen_last = jnp.where(jnp.logical_and(is_last, col >= 2 * BLK), NEG, 0.0)
    nqb = TM // BLK
    for qb in range(nqb):
        for kh in range(N_KV):
            qs = jnp.concatenate(
                [q_ref[qb * BLK:(qb + 1) * BLK, (kh * GRP + g) * HEAD_DIM:(kh * GRP + g + 1) * HEAD_DIM]
                 for g in range(GRP)], axis=0)
            kb = kvbuf[qb * BLK:qb * BLK + 3 * BLK, kh * HEAD_DIM:(kh + 1) * HEAD_DIM]
            vb = kvbuf[qb * BLK:qb * BLK + 3 * BLK, KV_W + kh * HEAD_DIM:KV_W + (kh + 1) * HEAD_DIM]
            s = lax.dot_general(qs, kb, (((1,), (1,)), ((), ())), preferred_element_type=F32)
            s = s + bias_ref[kh]
            if qb == 0:
                s = s + pen_first
            if qb == nqb - 1:
                s = s + pen_last
            sk = sink_ref[kh]
            m = jnp.maximum(jnp.max(s, axis=-1, keepdims=True), sk)
            p = jnp.exp2(s - m)
            den = jnp.sum(p, axis=-1, keepdims=True) + jnp.exp2(sk - m)
            o = jnp.dot(p.astype(BF16), vb, preferred_element_type=F32) / den
            for g in range(GRP):
                c0 = A_W + (kh * GRP + g) * HEAD_DIM
                mixed[qb * BLK:(qb + 1) * BLK, c0:c0 + HEAD_DIM] = o[g * BLK:(g + 1) * BLK].astype(BF16)

    mix = jnp.dot(mixed[...], wout_ref[...], preferred_element_type=F32)
    if layer0:
        x = jnp.where(i < npt, xp_ref[...], xs_ref[...])
    else:
        x = x_ref[...]
    xmid = x + mod_ref[:, 2 * D:3 * D] * mix
    xmid_out[...] = xmid
    h2 = _rms_mod(xmid, gf_ref[...], mod_ref[:, 4 * D:5 * D], mod_ref[:, 3 * D:4 * D])
    _tile_to_rows(h2r_out, h2, TM)

    lt = lax.dot_general(wrt_ref[...], h2.astype(BF16), (((1,), (1,)), ((), ())),
                         preferred_element_type=F32) + br_ref[...]
    bucket = _route_rows(lt)
    rows = lax.broadcasted_iota(jnp.int32, (RROWS, TM), 0).astype(F32)
    oh = jnp.where(rows == bucket, 1.0, 0.0)
    before = jnp.dot(oh.astype(BF16), tri_ref[...], preferred_element_type=F32)
    rank = jnp.sum(oh * (before + carry[:, 0:1]), axis=0, keepdims=True)
    carry[...] = carry[...] + jnp.sum(oh, axis=1, keepdims=True)
    cnt_out[...] = carry[...]
    zrow = jnp.zeros((ROUTE_ROWS - 2, TM), F32)
    route_out[...] = jnp.concatenate([bucket, rank, zrow], axis=0)


def _mix(layer, layer0, x_in, a, q, kv, cc, mod4, norm_ffn_g, caw, cab, lng, lnb, ccw, w_out_b,
         wrt, br, bias, sinkc, tri, T, npt, tps, tp, ss):
    nt = T // TM
    seq = lambda i: _seq_of_tile(i, npt, tps)
    row = lambda i: (i, 0)
    hp = TM // HALO
    bp = TM // BLK
    prev = lambda per: (lambda i: (jnp.maximum(i * per - 1, 0), 0))
    nxt = lambda per, n: (lambda i: (jnp.minimum((i + 1) * per, n - 1), 0))
    const2 = lambda i: (0, 0)
    lsel = lambda i: (layer, 0, 0)
    if layer0:
        nst = x_in[1].shape[0] // TM
        x_specs = [pl.BlockSpec((TM, D), lambda i: (jnp.minimum(i, npt - 1), 0)),
                   pl.BlockSpec((TM, D), lambda i: (jnp.clip(i - npt, 0, nst - 1), 0))]
    else:
        x_specs = [pl.BlockSpec((TM, D), row)]
    in_specs = x_specs + [
        pl.BlockSpec((TM, A_W), row),
        pl.BlockSpec((HALO, A_W), prev(hp)), pl.BlockSpec((HALO, A_W), nxt(hp, T // HALO)),
        pl.BlockSpec((TM, ATT_W), row),
        pl.BlockSpec((TM, 2 * KV_W), row),
        pl.BlockSpec((BLK, 2 * KV_W), prev(bp)), pl.BlockSpec((BLK, 2 * KV_W), nxt(bp, T // BLK)),
        pl.BlockSpec((TM, 2 * C_W), row),
        pl.BlockSpec((HALO, 2 * C_W), prev(hp)), pl.BlockSpec((HALO, 2 * C_W), nxt(hp, T // HALO)),
        pl.BlockSpec((None, None, 1, 6 * D), lambda i: (layer, seq(i), 0, 0)),
        pl.BlockSpec((None, 1, D), lsel),
        pl.BlockSpec((None, 32, A_W), lsel),
        pl.BlockSpec((None, 1, A_W), lsel), pl.BlockSpec((None, 1, A_W), lsel),
        pl.BlockSpec((None, 1, A_W), lsel),
        pl.BlockSpec((None, 8, C_W), lsel),
        _resident((None, D, D), lsel),
        pl.BlockSpec((None, RROWS, D), lsel),
        pl.BlockSpec((None, RROWS, 1), lsel),
        pl.BlockSpec((N_KV, GRP * BLK, 3 * BLK), lambda i: (0, 0, 0)),
        pl.BlockSpec((None, N_KV, GRP * BLK, 1), lambda i: (layer, 0, 0, 0)),
        pl.BlockSpec((TM, TM), const2),
    ]
    out_shape = [jax.ShapeDtypeStruct((T, D), F32), jax.ShapeDtypeStruct((T * LP, 128), F32),
                 jax.ShapeDtypeStruct((ROUTE_ROWS, T), F32), jax.ShapeDtypeStruct((RROWS, 128), F32)]
    out_specs = [pl.BlockSpec((TM, D), row), pl.BlockSpec((TM * LP, 128), row),
                 pl.BlockSpec((ROUTE_ROWS, TM), lambda i: (0, i)), pl.BlockSpec((RROWS, 128), const2)]
    scratch = [pltpu.VMEM((8, TM + 2 * HALO, A_W), F32), pltpu.VMEM((TM + 2 * HALO, C_W), F32),
               pltpu.VMEM((TM + 2 * BLK, 2 * KV_W), BF16), pltpu.VMEM((TM, D), BF16),
               pltpu.VMEM((RROWS, 128), F32)]
    args = list(x_in) + [a, a, a, q, kv, kv, kv, cc, cc, cc, mod4, norm_ffn_g, caw, cab, lng, lnb, ccw,
                         w_out_b, wrt, br, bias, sinkc, tri]
    return pl.pallas_call(
        functools.partial(_mix_body, layer0, npt, tp, ss),
        grid=(nt,), in_specs=in_specs, out_specs=out_specs, out_shape=out_shape,
        scratch_shapes=scratch, compiler_params=_cparams(),
    )(*args)


def _invert_body(pos_ref, src_ref):
    def zero(k, c):
        src_ref[k] = 0
        return c

    lax.fori_loop(0, src_ref.shape[0], zero, 0, unroll=8)

    def put(t, c):
        src_ref[pos_ref[t]] = t
        return c

    lax.fori_loop(0, pos_ref.shape[0], put, 0, unroll=8)


def _invert(pos, n_pad):
    return pl.pallas_call(
        _invert_body,
        grid_spec=pltpu.PrefetchScalarGridSpec(
            num_scalar_prefetch=1, grid=(1,), in_specs=[],
            out_specs=pl.BlockSpec(memory_space=pltpu.SMEM)),
        out_shape=jax.ShapeDtypeStruct((n_pad,), jnp.int32),
    )(pos)


def _expert_half(x, wgu_ref, wd_ref):
    gu = jnp.dot(x, wgu_ref[...], preferred_element_type=F32)
    gate = gu[:, :D_EXP]
    he = gate * jax.nn.sigmoid(gate) * gu[:, D_EXP:]
    return jnp.dot(he.astype(BF16), wd_ref[...], preferred_element_type=F32)


def _combine_weights(x, wr_ref, br_ref, g, a, b):
    lc = jnp.dot(x, wr_ref[...], preferred_element_type=F32) + br_ref[...]
    lane = lax.broadcasted_iota(jnp.int32, (1, RLANES), 1)
    rsum = lambda v: jnp.sum(v, axis=-1, keepdims=True)
    gmask = lane < N_GROUPS
    mg = jnp.max(jnp.where(gmask, lc, NEG), axis=-1, keepdims=True)
    zg = rsum(jnp.exp(jnp.where(gmask, lc - mg, NEG)))
    gp = jnp.exp(rsum(jnp.where(lane == g, lc, 0.0)) - mg) / zg
    e0 = N_GROUPS + EPG * g
    emask = jnp.logical_and(lane >= e0, lane < e0 + EPG)
    ms = jnp.max(jnp.where(emask, lc, NEG), axis=-1, keepdims=True)
    es = jnp.exp(jnp.where(emask, lc - ms, NEG))
    zs = rsum(es)
    pa = rsum(jnp.where(lane == e0 + a, es, 0.0)) / zs
    pb = rsum(jnp.where(lane == e0 + b, es, 0.0)) / zs
    den = pa + pb
    return gp * (pa / den), gp * (pb / den)


def _expert_body(ea_ref, eb_ref, nv_ref, src_ref, h_hbm, wr_ref, br_ref, wgu_a, wd_a, wgu_b, wd_b, y_ref,
                 xbuf, xsem):
    j = pl.program_id(0)
    nv = nv_ref[j]
    slot = _gather_tile(src_ref, h_hbm, xbuf, xsem, TME, lambda step: nv_ref[step] > 0)

    @pl.when(nv > 0)
    def _():
        x = _rows_to_tile(xbuf.at[slot], TME).astype(BF16)
        ea = ea_ref[j]
        w_a, w_b = _combine_weights(x, wr_ref, br_ref, ea // EPG, ea % EPG, eb_ref[j] % EPG)
        y = w_a * _expert_half(x, wgu_a, wd_a) + w_b * _expert_half(x, wgu_b, wd_b)
        _tile_to_rows(y_ref, y, TME)

    @pl.when(nv == 0)
    def _():
        y_ref[...] = jnp.zeros_like(y_ref)


def _experts(layer, tile_ea, tile_eb, tile_nv, src, h2r, wr, brl, wgu_b, wd_b, n_pad):
    nt = n_pad // TME
    wsel = lambda ref_idx: (lambda j, ea, eb, nv, src: (layer, (ea, eb)[ref_idx][j], 0, 0))
    lsel = lambda j, ea, eb, nv, src: (layer, 0, 0)
    return pl.pallas_call(
        _expert_body,
        grid_spec=pltpu.PrefetchScalarGridSpec(
            num_scalar_prefetch=4, grid=(nt,),
            in_specs=[pl.BlockSpec(memory_space=pl.ANY),
                      pl.BlockSpec((None, D, RLANES), lsel),
                      pl.BlockSpec((None, 1, RLANES), lsel),
                      pl.BlockSpec((None, None, D, 2 * D_EXP), wsel(0)),
                      pl.BlockSpec((None, None, D_EXP, D), wsel(0)),
                      pl.BlockSpec((None, None, D, 2 * D_EXP), wsel(1)),
                      pl.BlockSpec((None, None, D_EXP, D), wsel(1))],
            out_specs=pl.BlockSpec((TME * LP, 128), lambda j, ea, eb, nv, src: (j, 0)),
            scratch_shapes=_ring_scratch(TME)),
        out_shape=jax.ShapeDtypeStruct((n_pad * LP, 128), F32),
        compiler_params=_cparams(),
    )(tile_ea, tile_eb, tile_nv, src, h2r, wr, brl, wgu_b, wd_b, wgu_b, wd_b)


def _final_body(tile0, pos_ref, xm_ref, ys_ref, mod_ref, g_ref, o_ref, ybuf, ysem):
    slot = _gather_tile(pos_ref, ys_ref, ybuf, ysem, TM, lambda step: step >= 0, base=tile0 * TM)
    x = xm_ref[...] + mod_ref[:, 5 * D:6 * D] * _rows_to_tile(ybuf.at[slot], TM)
    o_ref[...] = x * lax.rsqrt(jnp.mean(x * x, axis=-1, keepdims=True) + EPS) * g_ref[...]


def _final(layer, pos, xmid, ysort, mod4, final_g, tile0, ntiles, npt, tps):
    seq = lambda i: _seq_of_tile(i + tile0, npt, tps)
    return pl.pallas_call(
        functools.partial(_final_body, tile0),
        grid_spec=pltpu.PrefetchScalarGridSpec(
            num_scalar_prefetch=1, grid=(ntiles,),
            in_specs=[pl.BlockSpec((TM, D), lambda i, pos: (i + tile0, 0)),
                      pl.BlockSpec(memory_space=pl.ANY),
                      pl.BlockSpec((None, None, 1, 6 * D), lambda i, pos: (layer, seq(i), 0, 0)),
                      pl.BlockSpec((1, D), lambda i, pos: (0, 0))],
            out_specs=pl.BlockSpec((TM, D), lambda i, pos: (i, 0)),
            scratch_shapes=_ring_scratch(TM)),
        out_shape=jax.ShapeDtypeStruct((ntiles * TM, D), F32),
        compiler_params=_cparams(),
    )(pos, xmid, ysort, mod4, final_g)


def _bucket_tables():
    ea, eb = [], []
    for g in range(N_GROUPS):
        for a in range(EPG):
            for b in range(a + 1, EPG):
                ea.append(g * EPG + a)
                eb.append(g * EPG + b)
    return np.asarray(ea, np.int32), np.asarray(eb, np.int32)


def _plan(route, counts, n_pad):
    bucket = route[0].astype(jnp.int32)
    rank = route[1].astype(jnp.int32)
    cnt = counts[:N_BUCKET, 0].astype(jnp.int32)
    padded = (cnt + TME - 1) // TME * TME
    ends = jnp.cumsum(padded)
    starts = ends - padded
    pos = starts[bucket] + rank
    nt = n_pad // TME
    tile_start = jnp.arange(nt, dtype=jnp.int32) * TME
    n_used = ends[-1] // TME
    tb = jnp.sum(tile_start[:, None] >= ends[None, :], axis=1).astype(jnp.int32)
    last_b = jnp.max(jnp.where(cnt > 0, jnp.arange(N_BUCKET, dtype=jnp.int32), 0))
    used = jnp.arange(nt, dtype=jnp.int32) < n_used
    tb = jnp.where(used, jnp.minimum(tb, N_BUCKET - 1), last_b)
    ea_tab, eb_tab = _bucket_tables()
    tile_ea = jnp.asarray(ea_tab)[tb]
    tile_eb = jnp.asarray(eb_tab)[tb]
    tile_nv = jnp.where(used, jnp.clip(starts[tb] + cnt[tb] - tile_start, 0, TME), 0).astype(jnp.int32)
    return pos, tile_ea, tile_eb, tile_nv


def _attn_bias():
    r = np.arange(BLK)[:, None]
    c = np.arange(3 * BLK)[None, :]
    dist = np.abs(c - BLK - r).astype(np.float32)
    slopes = 2.0 ** (-8.0 * np.arange(1, N_HEADS + 1, dtype=np.float32) / N_HEADS)
    tab = np.where(dist[None] <= WINDOW, -LOG2E * slopes[:, None, None] * dist[None], NEG).astype(np.float32)
    return jnp.asarray(tab.reshape(N_KV, GRP * BLK, 3 * BLK))


def kernel(x_prompt, x_sample, c_prompt, c_sample, norm_mix_g, norm_ffn_g, w_ada, b_ada, w_in, w_out,
           conv_a_w, conv_a_b, ln_a_g, ln_a_b, attn_sink, conv_c_w, w_router_group, b_router_group,
           w_router_expert, b_router_expert, w_gate, w_up, w_down, final_norm_g):
    depth = w_in.shape[0]
    bp, sp, _ = x_prompt.shape
    bs, ss, _ = x_sample.shape
    assert bp == 1 and sp % TM == 0 and ss % TM == 0 and (bp + bs) <= 8
    tp = bp * sp
    T = tp + bs * ss
    npt, tps = tp // TM, ss // TM
    n_pad = T + N_BUCKET * TME

    xp = x_prompt.reshape(tp, D)
    xs = x_sample.reshape(bs * ss, D)
    c_all = jnp.concatenate([c_prompt, c_sample, jnp.zeros((8 - bp - bs, D), F32)], axis=0)
    mod4 = _ada(c_all, w_ada, b_ada).reshape(depth, 8, 1, 6 * D)

    w_in_b = w_in.astype(BF16)
    w_out_b = w_out.astype(BF16)
    wgu_b = jnp.concatenate([w_gate, w_up], axis=-1).astype(BF16)
    wd_b = w_down.astype(BF16)
    wrt = jnp.concatenate([w_router_group, w_router_expert,
                           jnp.zeros((depth, D, RROWS - N_GROUPS - N_EXP), F32)], axis=-1)
    wr_cols = jnp.pad(wrt, ((0, 0), (0, 0), (0, RLANES - RROWS))).astype(BF16)
    wrt = jnp.swapaxes(wrt, 1, 2).astype(BF16)
    br = jnp.concatenate([b_router_group, b_router_expert,
                          jnp.zeros((depth, RROWS - N_GROUPS - N_EXP), F32)], axis=-1)
    br_lanes = jnp.pad(br, ((0, 0), (0, RLANES - RROWS)))[:, None, :]
    br = br[..., None]
    caw = jnp.pad(conv_a_w, ((0, 0), (0, 32 - CONV_A), (0, 0)))
    ccw = jnp.pad(conv_c_w, ((0, 0), (0, 8 - CONV_C), (0, 0)))
    r3 = lambda v: v.reshape(depth, 1, -1)
    sinkc = jnp.broadcast_to((LOG2E * attn_sink).reshape(depth, N_KV, GRP, 1, 1),
                             (depth, N_KV, GRP, BLK, 1)).reshape(depth, N_KV, GRP * BLK, 1)
    bias = _attn_bias()
    tri = jnp.asarray(np.triu(np.ones((TM, TM), np.float32), 1)).astype(BF16)

    x_mid = ysort = pos = None
    for l in range(depth):
        if l == 0:
            a, q, kv, cc = _pre(l, False, (xp, xs), mod4, r3(norm_mix_g), w_in_b, T, npt, tps)
            x_in = (xp, xs)
        else:
            x, a, q, kv, cc = _pre(l, True, (x_mid, ysort), mod4, r3(norm_mix_g), w_in_b, T, npt, tps, pos)
            x_in = (x,)
        x_mid, h2r, route, counts = _mix(
            l, l == 0, x_in, a, q, kv, cc, mod4, r3(norm_ffn_g), caw, r3(conv_a_b), r3(ln_a_g),
            r3(ln_a_b), ccw, w_out_b, wrt, br, bias, sinkc, tri, T, npt, tps, tp, ss)
        pos, tile_ea, tile_eb, tile_nv = _plan(route, counts, n_pad)
        src = _invert(pos, n_pad)
        ysort = _experts(l, tile_ea, tile_eb, tile_nv, src, h2r, wr_cols, br_lanes, wgu_b, wd_b, n_pad)

    fg = final_norm_g.reshape(1, D)
    y_p = _final(depth - 1, pos, x_mid, ysort, mod4, fg, 0, npt, npt, tps)
    y_s = _final(depth - 1, pos, x_mid, ysort, mod4, fg, npt, bs * tps, npt, tps)
    return y_p.reshape(bp, sp, D), y_s.reshape(bs, ss, D)
```

```python
import functools

import jax
import jax.numpy as jnp
import numpy as np
from jax import lax
from jax.experimental import pallas as pl
from jax.experimental.pallas import tpu as pltpu

F32 = jnp.float32
BF16 = jnp.bfloat16

D = 2048
N_HEADS = 8
HEAD_DIM = 128
N_KV = 2
GRP = N_HEADS // N_KV
A_W = 512
C_W = 512
ATT_W = N_HEADS * HEAD_DIM
KV_W = N_KV * HEAD_DIM
D_IN = 4096
CONV_A = 31
CONV_C = 3
WINDOW = 128
BLK = 128
N_GROUPS = 4
EPG = 4
N_EXP = N_GROUPS * EPG
D_EXP = 512
N_PAIR = 6
N_BUCKET = N_GROUPS * N_PAIR
EPS = 1e-6
NEG = -1e30
LOG2E = 1.4426950408889634

TM = 256
TME = 256
HALO = 16
LT = D // 128
LP = LT + 1
ROUTE_ROWS = 8
RROWS = 32
RLANES = 128
VMEM_LIMIT = 56 * 1024 * 1024


def _cparams(sem=("arbitrary",)):
    return pltpu.CompilerParams(dimension_semantics=sem, vmem_limit_bytes=VMEM_LIMIT)


def _resident(shape, index_map):
    return pl.BlockSpec(shape, index_map, pipeline_mode=pl.Buffered(1))


def _ada_body(c_ref, w_ref, b_ref, o_ref):
    c = c_ref[...]
    sc = c * jax.nn.sigmoid(c)
    o_ref[...] = jnp.dot(sc, w_ref[...], preferred_element_type=F32,
                         precision=lax.Precision.HIGHEST) + b_ref[...]


def _ada(c_all, w_ada, b_ada):
    depth = w_ada.shape[0]
    tn = 1024
    return pl.pallas_call(
        _ada_body,
        grid=(depth, 6 * D // tn),
        in_specs=[pl.BlockSpec((8, D), lambda l, j: (0, 0)),
                  pl.BlockSpec((None, D, tn), lambda l, j: (l, 0, j)),
                  pl.BlockSpec((None, 1, tn), lambda l, j: (l, 0, j))],
        out_specs=pl.BlockSpec((None, 8, tn), lambda l, j: (l, 0, j)),
        out_shape=jax.ShapeDtypeStruct((depth, 8, 6 * D), F32),
        compiler_params=_cparams(("arbitrary", "arbitrary")),
    )(c_all, w_ada, b_ada.reshape(depth, 1, 6 * D))


def _seq_of_tile(i, npt, tps):
    return jnp.where(i < npt, 0, 1 + (i - npt) // tps)


def _rows_to_tile(ref, n):
    return jnp.concatenate([ref[pl.ds(c, n, stride=LP), :] for c in range(LT)], axis=1)


def _tile_to_rows(ref, val, n):
    for c in range(LT):
        ref[pl.ds(c, n, stride=LP), :] = val[:, c * 128:(c + 1) * 128]
    ref[pl.ds(LT, n, stride=LP), :] = jnp.zeros((n, 128), F32)


def _row_dma(src, dst, s, d, sem):
    return pltpu.make_async_copy(src.at[pl.ds(s * LP, LP)], dst.at[pl.ds(d * LP, LP)], sem)


def _gather_tile(idx_ref, src_hbm, buf, sem, n, live, base=0):
    i = pl.program_id(0)
    slot = i % 2

    def start(step, s):
        @pl.when(live(step))
        def _():
            def issue(r, c):
                _row_dma(src_hbm, buf.at[s], idx_ref[base + step * n + r], r, sem.at[s]).start()
                return c
            lax.fori_loop(0, n, issue, 0, unroll=8)

    @pl.when(i == 0)
    def _():
        start(i, slot)

    @pl.when(i + 1 < pl.num_programs(0))
    def _():
        start(i + 1, 1 - slot)

    @pl.when(live(i))
    def _():
        def drain(r, c):
            _row_dma(src_hbm, buf.at[slot], 0, 0, sem.at[slot]).wait()
            return c
        lax.fori_loop(0, n, drain, 0, unroll=8)

    return slot


def _rms_mod(x, g, scale, shift):
    y = x * lax.rsqrt(jnp.mean(x * x, axis=-1, keepdims=True) + EPS) * g
    return y * (1.0 + scale) + shift


def _pre_body(fused, npt, *refs):
    if fused:
        (pos_ref, xm_ref, ys_ref, modp_ref, mod_ref, g_ref, w_ref,
         x_out, a_out, q_out, kv_out, cc_out, ybuf, ysem) = refs
        slot = _gather_tile(pos_ref, ys_ref, ybuf, ysem, TM, lambda step: step >= 0)
        x = xm_ref[...] + modp_ref[:, 5 * D:6 * D] * _rows_to_tile(ybuf.at[slot], TM)
        x_out[...] = x
    else:
        xp_ref, xs_ref, mod_ref, g_ref, w_ref, a_out, q_out, kv_out, cc_out = refs
        x = jnp.where(pl.program_id(0) < npt, xp_ref[...], xs_ref[...])
    h = _rms_mod(x, g_ref[...], mod_ref[:, D:2 * D], mod_ref[:, 0:D]).astype(BF16)

    za = jnp.dot(h, w_ref[:, 0:2 * A_W], preferred_element_type=F32)
    a_out[...] = (za[:, :A_W] * jax.nn.sigmoid(za[:, A_W:])).astype(BF16)
    o = 2 * A_W
    zq = jnp.dot(h, w_ref[:, o:o + ATT_W], preferred_element_type=F32)
    q_out[...] = (zq * (LOG2E * HEAD_DIM ** -0.5)).astype(BF16)
    o += ATT_W
    kv_out[...] = jnp.dot(h, w_ref[:, o:o + 2 * KV_W], preferred_element_type=F32).astype(BF16)
    o += 2 * KV_W
    zc = jnp.dot(h, w_ref[:, o:o + 3 * C_W], preferred_element_type=F32)
    cc_out[:, 0:C_W] = (zc[:, 2 * C_W:] * zc[:, :C_W]).astype(BF16)
    cc_out[:, C_W:] = zc[:, C_W:2 * C_W].astype(BF16)


def _ring_scratch(n):
    return [pltpu.VMEM((2, n * LP, 128), F32), pltpu.SemaphoreType.DMA((2,))]


def _pre(layer, fused, xs_in, mod4, norm_g, w_in_b, T, npt, tps, pos=None):
    nt = T // TM
    seq = lambda i: _seq_of_tile(i, npt, tps)
    row = lambda i, *_: (i, 0)
    mod_spec = lambda l: pl.BlockSpec((None, None, 1, 6 * D), lambda i, *_: (l, seq(i), 0, 0))
    if fused:
        x_specs = [pl.BlockSpec((TM, D), row), pl.BlockSpec(memory_space=pl.ANY), mod_spec(layer - 1)]
    else:
        nst = xs_in[1].shape[0] // TM
        x_specs = [pl.BlockSpec((TM, D), lambda i: (jnp.minimum(i, npt - 1), 0)),
                   pl.BlockSpec((TM, D), lambda i: (jnp.clip(i - npt, 0, nst - 1), 0))]
    in_specs = x_specs + [mod_spec(layer),
                          pl.BlockSpec((None, 1, D), lambda i, *_: (layer, 0, 0)),
                          _resident((None, D, D_IN), lambda i, *_: (layer, 0, 0))]
    out_shape = [jax.ShapeDtypeStruct((T, A_W), BF16), jax.ShapeDtypeStruct((T, ATT_W), BF16),
                 jax.ShapeDtypeStruct((T, 2 * KV_W), BF16), jax.ShapeDtypeStruct((T, 2 * C_W), BF16)]
    out_specs = [pl.BlockSpec((TM, A_W), row), pl.BlockSpec((TM, ATT_W), row),
                 pl.BlockSpec((TM, 2 * KV_W), row), pl.BlockSpec((TM, 2 * C_W), row)]
    if fused:
        out_shape = [jax.ShapeDtypeStruct((T, D), F32)] + out_shape
        out_specs = [pl.BlockSpec((TM, D), row)] + out_specs
    args = ([pos] if fused else []) + list(xs_in) + ([mod4] if fused else []) + [mod4, norm_g, w_in_b]
    return pl.pallas_call(
        functools.partial(_pre_body, fused, npt),
        grid_spec=pltpu.PrefetchScalarGridSpec(
            num_scalar_prefetch=1 if fused else 0, grid=(nt,), in_specs=in_specs, out_specs=out_specs,
            scratch_shapes=_ring_scratch(TM) if fused else []),
        out_shape=out_shape, compiler_params=_cparams(),
    )(*args)


def _route_rows(lt):
    lg = [lt[r:r + 1, :] for r in range(N_GROUPS)]
    mg = functools.reduce(jnp.maximum, lg)
    g = jnp.where(lg[0] == mg, 0.0, jnp.where(lg[1] == mg, 1.0, jnp.where(lg[2] == mg, 2.0, 3.0)))
    le = [lt[N_GROUPS + e:N_GROUPS + e + 1, :] for e in range(N_EXP)]
    sel = [jnp.where(g == 0.0, le[j], jnp.where(g == 1.0, le[EPG + j],
                                                jnp.where(g == 2.0, le[2 * EPG + j], le[3 * EPG + j])))
           for j in range(EPG)]
    ms = functools.reduce(jnp.maximum, sel)
    es = [jnp.exp(s - ms) for s in sel]
    zs = functools.reduce(jnp.add, es)
    p = [e / zs for e in es]
    v1 = functools.reduce(jnp.maximum, p)
    i1 = jnp.where(p[0] == v1, 0.0, jnp.where(p[1] == v1, 1.0, jnp.where(p[2] == v1, 2.0, 3.0)))
    p2 = [jnp.where(i1 == float(j), -1.0, p[j]) for j in range(EPG)]
    v2 = functools.reduce(jnp.maximum, p2)
    i2 = jnp.where(p2[0] == v2, 0.0, jnp.where(p2[1] == v2, 1.0, jnp.where(p2[2] == v2, 2.0, 3.0)))
    a = jnp.minimum(i1, i2)
    b = jnp.maximum(i1, i2)
    pair = jnp.where(a == 0.0, b - 1.0, jnp.where(a == 1.0, b + 1.0, 5.0))
    return g * float(N_PAIR) + pair


def _mix_body(layer0, npt, tp, ss, *refs):
    if layer0:
        xp_ref, xs_ref = refs[0], refs[1]
        refs = refs[2:]
    else:
        x_ref = refs[0]
        refs = refs[1:]
    (a_cur, a_prev, a_next, q_ref, kv_cur, kv_prev, kv_next, cc_cur, cc_prev, cc_next,
     mod_ref, gf_ref, caw_ref, cab_ref, lng_ref, lnb_ref, ccw_ref, wout_ref, wrt_ref, br_ref,
     bias_ref, sink_ref, tri_ref,
     xmid_out, h2r_out, route_out, cnt_out,
     abuf, cbuf, kvbuf, mixed, carry) = refs

    i = pl.program_id(0)
    t0 = i * TM
    in_p = t0 < tp
    p0 = jnp.where(in_p, t0, (t0 - tp) % ss)
    slen = jnp.where(in_p, tp, ss)
    is_first = p0 == 0
    is_last = p0 + TM == slen

    @pl.when(i == 0)
    def _():
        carry[...] = jnp.zeros_like(carry)

    abuf[0, 0:HALO, :] = jnp.where(is_first, 0.0, a_prev[...].astype(F32))
    abuf[0, HALO:HALO + TM, :] = a_cur[...].astype(F32)
    abuf[0, HALO + TM:2 * HALO + TM, :] = jnp.where(is_last, 0.0, a_next[...].astype(F32))
    for k in range(1, 8):
        abuf[k, 0:TM + 24, :] = abuf[0, k:k + TM + 24, :]
    rc = 64

    def conv_a_chunk(r):
        acc = jnp.zeros((rc // 8, 8, A_W), F32)
        for j in range(CONV_A):
            s = HALO - (CONV_A - 1) // 2 + j
            o = r * rc + s // 8 * 8
            acc = acc + caw_ref[j] * abuf[s % 8, o:o + rc, :].reshape(rc // 8, 8, A_W)
        acc = acc.reshape(rc, A_W) + cab_ref[...]
        mu = jnp.mean(acc, axis=-1, keepdims=True)
        xc = acc - mu
        var = jnp.mean(xc * xc, axis=-1, keepdims=True)
        yn = xc * lax.rsqrt(var + EPS) * lng_ref[...] + lnb_ref[...]
        mixed[r * rc:(r + 1) * rc, 0:A_W] = (yn * jax.nn.sigmoid(yn)).astype(BF16)

    cbuf[0:HALO, :] = jnp.where(is_first, 0.0, cc_prev[:, 0:C_W].astype(F32))
    cbuf[HALO:HALO + TM, :] = cc_cur[:, 0:C_W].astype(F32)
    cbuf[HALO + TM:2 * HALO + TM, :] = jnp.where(is_last, 0.0, cc_next[:, 0:C_W].astype(F32))

    def conv_c_chunk(r):
        acc = jnp.zeros((rc, C_W), F32)
        for j in range(CONV_C):
            s = r * rc + HALO - (CONV_C - 1) // 2 + j
            acc = acc + ccw_ref[j:j + 1, :] * cbuf[s:s + rc, :]
        gate = cc_cur[r * rc:(r + 1) * rc, C_W:2 * C_W].astype(F32)
        mixed[r * rc:(r + 1) * rc, A_W + ATT_W:] = (gate * acc).astype(BF16)

    kvbuf[0:BLK, :] = kv_prev[...]
    kvbuf[BLK:BLK + TM, :] = kv_cur[...]
    kvbuf[BLK + TM:2 * BLK + TM, :] = kv_next[...]
    col = lax.broadcasted_iota(jnp.int32, (1, 3 * BLK), 1)
    pen_first = jnp.where(jnp.logical_and(is_first, col < BLK), NEG, 0.0)
    pen_last = jnp.where(jnp.logical_and(is_last, col >= 2 * BLK), NEG, 0.0)
    nqb = TM // BLK

    def attn_block(qb, kh):
        qs = jnp.concatenate(
            [q_ref[qb * BLK:(qb + 1) * BLK, (kh * GRP + g) * HEAD_DIM:(kh * GRP + g + 1) * HEAD_DIM]
             for g in range(GRP)], axis=0)
        kb = kvbuf[qb * BLK:qb * BLK + 3 * BLK, kh * HEAD_DIM:(kh + 1) * HEAD_DIM]
        vb = kvbuf[qb * BLK:qb * BLK + 3 * BLK, KV_W + kh * HEAD_DIM:KV_W + (kh + 1) * HEAD_DIM]
        s = lax.dot_general(qs, kb, (((1,), (1,)), ((), ())), preferred_element_type=F32)
        s = s + bias_ref[kh]
        if qb == 0:
            s = s + pen_first
        if qb == nqb - 1:
            s = s + pen_last
        sk = sink_ref[kh]
        m = jnp.maximum(jnp.max(s, axis=-1, keepdims=True), sk)
        p = jnp.exp2(s - m)
        den = jnp.sum(p, axis=-1, keepdims=True) + jnp.exp2(sk - m)
        o = jnp.dot(p.astype(BF16), vb, preferred_element_type=F32) / den
        for g in range(GRP):
            c0 = A_W + (kh * GRP + g) * HEAD_DIM
            mixed[qb * BLK:(qb + 1) * BLK, c0:c0 + HEAD_DIM] = o[g * BLK:(g + 1) * BLK].astype(BF16)

    blocks = [(qb, kh) for qb in range(nqb) for kh in range(N_KV)]
    for r in range(TM // rc):
        attn_block(*blocks[r])
        conv_a_chunk(r)
    for k in range(TM // rc, len(blocks)):
        attn_block(*blocks[k])
    for r in range(TM // rc):
        conv_c_chunk(r)

    mix = jnp.dot(mixed[...], wout_ref[...], preferred_element_type=F32)
    if layer0:
        x = jnp.where(i < npt, xp_ref[...], xs_ref[...])
    else:
        x = x_ref[...]
    xmid = x + mod_ref[:, 2 * D:3 * D] * mix
    xmid_out[...] = xmid
    h2 = _rms_mod(xmid, gf_ref[...], mod_ref[:, 4 * D:5 * D], mod_ref[:, 3 * D:4 * D])
    _tile_to_rows(h2r_out, h2, TM)

    lt = lax.dot_general(wrt_ref[...], h2.astype(BF16), (((1,), (1,)), ((), ())),
                         preferred_element_type=F32) + br_ref[...]
    bucket = _route_rows(lt)
    rows = lax.broadcasted_iota(jnp.int32, (RROWS, TM), 0).astype(F32)
    oh = jnp.where(rows == bucket, 1.0, 0.0)
    before = jnp.dot(oh.astype(BF16), tri_ref[...], preferred_element_type=F32)
    rank = jnp.sum(oh * (before + carry[:, 0:1]), axis=0, keepdims=True)
    carry[...] = carry[...] + jnp.sum(oh, axis=1, keepdims=True)
    cnt_out[...] = carry[...]
    zrow = jnp.zeros((ROUTE_ROWS - 2, TM), F32)
    route_out[...] = jnp.concatenate([bucket, rank, zrow], axis=0)


def _mix(layer, layer0, x_in, a, q, kv, cc, mod4, norm_ffn_g, caw, cab, lng, lnb, ccw, w_out_b,
         wrt, br, bias, sinkc, tri, T, npt, tps, tp, ss):
    nt = T // TM
    seq = lambda i: _seq_of_tile(i, npt, tps)
    row = lambda i: (i, 0)
    hp = TM // HALO
    bp = TM // BLK
    prev = lambda per: (lambda i: (jnp.maximum(i * per - 1, 0), 0))
    nxt = lambda per, n: (lambda i: (jnp.minimum((i + 1) * per, n - 1), 0))
    const2 = lambda i: (0, 0)
    lsel = lambda i: (layer, 0, 0)
    if layer0:
        nst = x_in[1].shape[0] // TM
        x_specs = [pl.BlockSpec((TM, D), lambda i: (jnp.minimum(i, npt - 1), 0)),
                   pl.BlockSpec((TM, D), lambda i: (jnp.clip(i - npt, 0, nst - 1), 0))]
    else:
        x_specs = [pl.BlockSpec((TM, D), row)]
    in_specs = x_specs + [
        pl.BlockSpec((TM, A_W), row),
        pl.BlockSpec((HALO, A_W), prev(hp)), pl.BlockSpec((HALO, A_W), nxt(hp, T // HALO)),
        pl.BlockSpec((TM, ATT_W), row),
        pl.BlockSpec((TM, 2 * KV_W), row),
        pl.BlockSpec((BLK, 2 * KV_W), prev(bp)), pl.BlockSpec((BLK, 2 * KV_W), nxt(bp, T // BLK)),
        pl.BlockSpec((TM, 2 * C_W), row),
        pl.BlockSpec((HALO, 2 * C_W), prev(hp)), pl.BlockSpec((HALO, 2 * C_W), nxt(hp, T // HALO)),
        pl.BlockSpec((None, None, 1, 6 * D), lambda i: (layer, seq(i), 0, 0)),
        pl.BlockSpec((None, 1, D), lsel),
        pl.BlockSpec((None, CONV_A, 8, A_W), lambda i: (layer, 0, 0, 0)),
        pl.BlockSpec((None, 1, A_W), lsel), pl.BlockSpec((None, 1, A_W), lsel),
        pl.BlockSpec((None, 1, A_W), lsel),
        pl.BlockSpec((None, 8, C_W), lsel),
        _resident((None, D, D), lsel),
        pl.BlockSpec((None, RROWS, D), lsel),
        pl.BlockSpec((None, RROWS, 1), lsel),
        pl.BlockSpec((N_KV, GRP * BLK, 3 * BLK), lambda i: (0, 0, 0)),
        pl.BlockSpec((None, N_KV, GRP * BLK, 1), lambda i: (layer, 0, 0, 0)),
        pl.BlockSpec((TM, TM), const2),
    ]
    out_shape = [jax.ShapeDtypeStruct((T, D), F32), jax.ShapeDtypeStruct((T * LP, 128), F32),
                 jax.ShapeDtypeStruct((ROUTE_ROWS, T), F32), jax.ShapeDtypeStruct((RROWS, 128), F32)]
    out_specs = [pl.BlockSpec((TM, D), row), pl.BlockSpec((TM * LP, 128), row),
                 pl.BlockSpec((ROUTE_ROWS, TM), lambda i: (0, i)), pl.BlockSpec((RROWS, 128), const2)]
    scratch = [pltpu.VMEM((8, TM + 2 * HALO, A_W), F32), pltpu.VMEM((TM + 2 * HALO, C_W), F32),
               pltpu.VMEM((TM + 2 * BLK, 2 * KV_W), BF16), pltpu.VMEM((TM, D), BF16),
               pltpu.VMEM((RROWS, 128), F32)]
    args = list(x_in) + [a, a, a, q, kv, kv, kv, cc, cc, cc, mod4, norm_ffn_g, caw, cab, lng, lnb, ccw,
                         w_out_b, wrt, br, bias, sinkc, tri]
    return pl.pallas_call(
        functools.partial(_mix_body, layer0, npt, tp, ss),
        grid=(nt,), in_specs=in_specs, out_specs=out_specs, out_shape=out_shape,
        scratch_shapes=scratch, compiler_params=_cparams(),
    )(*args)


def _invert_body(pos_ref, src_ref):
    def zero(k, c):
        src_ref[k] = 0
        return c

    lax.fori_loop(0, src_ref.shape[0], zero, 0, unroll=32)

    def put(t, c):
        src_ref[pos_ref[t]] = t
        return c

    lax.fori_loop(0, pos_ref.shape[0], put, 0, unroll=32)


def _invert(pos, n_pad):
    return pl.pallas_call(
        _invert_body,
        grid_spec=pltpu.PrefetchScalarGridSpec(
            num_scalar_prefetch=1, grid=(1,), in_specs=[],
            out_specs=pl.BlockSpec(memory_space=pltpu.SMEM)),
        out_shape=jax.ShapeDtypeStruct((n_pad,), jnp.int32),
    )(pos)


def _expert_half(x, wgu_ref, wd_ref):
    gu = jnp.dot(x, wgu_ref[...], preferred_element_type=F32)
    gate = gu[:, :D_EXP]
    he = gate * jax.nn.sigmoid(gate) * gu[:, D_EXP:]
    return jnp.dot(he.astype(BF16), wd_ref[...], preferred_element_type=F32)


def _combine_weights(x, wr_ref, br_ref, g, a, b):
    lc = jnp.dot(x, wr_ref[...], preferred_element_type=F32) + br_ref[...]
    lane = lax.broadcasted_iota(jnp.int32, (1, RLANES), 1)
    rsum = lambda v: jnp.sum(v, axis=-1, keepdims=True)
    gmask = lane < N_GROUPS
    mg = jnp.max(jnp.where(gmask, lc, NEG), axis=-1, keepdims=True)
    zg = rsum(jnp.exp(jnp.where(gmask, lc - mg, NEG)))
    gp = jnp.exp(rsum(jnp.where(lane == g, lc, 0.0)) - mg) / zg
    e0 = N_GROUPS + EPG * g
    emask = jnp.logical_and(lane >= e0, lane < e0 + EPG)
    ms = jnp.max(jnp.where(emask, lc, NEG), axis=-1, keepdims=True)
    es = jnp.exp(jnp.where(emask, lc - ms, NEG))
    zs = rsum(es)
    pa = rsum(jnp.where(lane == e0 + a, es, 0.0)) / zs
    pb = rsum(jnp.where(lane == e0 + b, es, 0.0)) / zs
    den = pa + pb
    return gp * (pa / den), gp * (pb / den)


def _expert_body(ea_ref, eb_ref, nv_ref, src_ref, h_hbm, wr_ref, br_ref, wgu_a, wd_a, wgu_b, wd_b, y_ref,
                 xbuf, xsem):
    j = pl.program_id(0)
    nv = nv_ref[j]
    slot = _gather_tile(src_ref, h_hbm, xbuf, xsem, TME, lambda step: nv_ref[step] > 0)

    @pl.when(nv > 0)
    def _():
        x = _rows_to_tile(xbuf.at[slot], TME).astype(BF16)
        ea = ea_ref[j]
        w_a, w_b = _combine_weights(x, wr_ref, br_ref, ea // EPG, ea % EPG, eb_ref[j] % EPG)
        y = w_a * _expert_half(x, wgu_a, wd_a) + w_b * _expert_half(x, wgu_b, wd_b)
        _tile_to_rows(y_ref, y, TME)

    @pl.when(nv == 0)
    def _():
        y_ref[...] = jnp.zeros_like(y_ref)


def _experts(layer, tile_ea, tile_eb, tile_nv, src, h2r, wr, brl, wgu_b, wd_b, n_pad):
    nt = n_pad // TME
    wsel = lambda ref_idx: (lambda j, ea, eb, nv, src: (layer, (ea, eb)[ref_idx][j], 0, 0))
    lsel = lambda j, ea, eb, nv, src: (layer, 0, 0)
    return pl.pallas_call(
        _expert_body,
        grid_spec=pltpu.PrefetchScalarGridSpec(
            num_scalar_prefetch=4, grid=(nt,),
            in_specs=[pl.BlockSpec(memory_space=pl.ANY),
                      pl.BlockSpec((None, D, RLANES), lsel),
                      pl.BlockSpec((None, 1, RLANES), lsel),
                      pl.BlockSpec((None, None, D, 2 * D_EXP), wsel(0)),
                      pl.BlockSpec((None, None, D_EXP, D), wsel(0)),
                      pl.BlockSpec((None, None, D, 2 * D_EXP), wsel(1)),
                      pl.BlockSpec((None, None, D_EXP, D), wsel(1))],
            out_specs=pl.BlockSpec((TME * LP, 128), lambda j, ea, eb, nv, src: (j, 0)),
            scratch_shapes=_ring_scratch(TME)),
        out_shape=jax.ShapeDtypeStruct((n_pad * LP, 128), F32),
        compiler_params=_cparams(),
    )(tile_ea, tile_eb, tile_nv, src, h2r, wr, brl, wgu_b, wd_b, wgu_b, wd_b)


def _final_body(tile0, pos_ref, xm_ref, ys_ref, mod_ref, g_ref, o_ref, ybuf, ysem):
    slot = _gather_tile(pos_ref, ys_ref, ybuf, ysem, TM, lambda step: step >= 0, base=tile0 * TM)
    x = xm_ref[...] + mod_ref[:, 5 * D:6 * D] * _rows_to_tile(ybuf.at[slot], TM)
    o_ref[...] = x * lax.rsqrt(jnp.mean(x * x, axis=-1, keepdims=True) + EPS) * g_ref[...]


def _final(layer, pos, xmid, ysort, mod4, final_g, tile0, ntiles, npt, tps):
    seq = lambda i: _seq_of_tile(i + tile0, npt, tps)
    return pl.pallas_call(
        functools.partial(_final_body, tile0),
        grid_spec=pltpu.PrefetchScalarGridSpec(
            num_scalar_prefetch=1, grid=(ntiles,),
            in_specs=[pl.BlockSpec((TM, D), lambda i, pos: (i + tile0, 0)),
                      pl.BlockSpec(memory_space=pl.ANY),
                      pl.BlockSpec((None, None, 1, 6 * D), lambda i, pos: (layer, seq(i), 0, 0)),
                      pl.BlockSpec((1, D), lambda i, pos: (0, 0))],
            out_specs=pl.BlockSpec((TM, D), lambda i, pos: (i, 0)),
            scratch_shapes=_ring_scratch(TM)),
        out_shape=jax.ShapeDtypeStruct((ntiles * TM, D), F32),
        compiler_params=_cparams(),
    )(pos, xmid, ysort, mod4, final_g)


def _bucket_tables():
    ea, eb = [], []
    for g in range(N_GROUPS):
        for a in range(EPG):
            for b in range(a + 1, EPG):
                ea.append(g * EPG + a)
                eb.append(g * EPG + b)
    return np.asarray(ea, np.int32), np.asarray(eb, np.int32)


def _plan(route, counts, n_pad):
    bucket = route[0].astype(jnp.int32)
    rank = route[1].astype(jnp.int32)
    cnt = counts[:N_BUCKET, 0].astype(jnp.int32)
    padded = (cnt + TME - 1) // TME * TME
    ends = jnp.cumsum(padded)
    starts = ends - padded
    pos = starts[bucket] + rank
    nt = n_pad // TME
    tile_start = jnp.arange(nt, dtype=jnp.int32) * TME
    n_used = ends[-1] // TME
    tb = jnp.sum(tile_start[:, None] >= ends[None, :], axis=1).astype(jnp.int32)
    last_b = jnp.max(jnp.where(cnt > 0, jnp.arange(N_BUCKET, dtype=jnp.int32), 0))
    used = jnp.arange(nt, dtype=jnp.int32) < n_used
    tb = jnp.where(used, jnp.minimum(tb, N_BUCKET - 1), last_b)
    ea_tab, eb_tab = _bucket_tables()
    tile_ea = jnp.asarray(ea_tab)[tb]
    tile_eb = jnp.asarray(eb_tab)[tb]
    tile_nv = jnp.where(used, jnp.clip(starts[tb] + cnt[tb] - tile_start, 0, TME), 0).astype(jnp.int32)
    return pos, tile_ea, tile_eb, tile_nv


def _attn_bias():
    r = np.arange(BLK)[:, None]
    c = np.arange(3 * BLK)[None, :]
    dist = np.abs(c - BLK - r).astype(np.float32)
    slopes = 2.0 ** (-8.0 * np.arange(1, N_HEADS + 1, dtype=np.float32) / N_HEADS)
    tab = np.where(dist[None] <= WINDOW, -LOG2E * slopes[:, None, None] * dist[None], NEG).astype(np.float32)
    return jnp.asarray(tab.reshape(N_KV, GRP * BLK, 3 * BLK))


def kernel(x_prompt, x_sample, c_prompt, c_sample, norm_mix_g, norm_ffn_g, w_ada, b_ada, w_in, w_out,
           conv_a_w, conv_a_b, ln_a_g, ln_a_b, attn_sink, conv_c_w, w_router_group, b_router_group,
           w_router_expert, b_router_expert, w_gate, w_up, w_down, final_norm_g):
    depth = w_in.shape[0]
    bp, sp, _ = x_prompt.shape
    bs, ss, _ = x_sample.shape
    assert bp == 1 and sp % TM == 0 and ss % TM == 0 and (bp + bs) <= 8
    tp = bp * sp
    T = tp + bs * ss
    npt, tps = tp // TM, ss // TM
    n_pad = T + N_BUCKET * TME

    xp = x_prompt.reshape(tp, D)
    xs = x_sample.reshape(bs * ss, D)
    c_all = jnp.concatenate([c_prompt, c_sample, jnp.zeros((8 - bp - bs, D), F32)], axis=0)
    mod4 = _ada(c_all, w_ada, b_ada).reshape(depth, 8, 1, 6 * D)

    w_in_b = w_in.astype(BF16)
    w_out_b = w_out.astype(BF16)
    wgu_b = jnp.concatenate([w_gate, w_up], axis=-1).astype(BF16)
    wd_b = w_down.astype(BF16)
    wrt = jnp.concatenate([w_router_group, w_router_expert,
                           jnp.zeros((depth, D, RROWS - N_GROUPS - N_EXP), F32)], axis=-1)
    wr_cols = jnp.pad(wrt, ((0, 0), (0, 0), (0, RLANES - RROWS))).astype(BF16)
    wrt = jnp.swapaxes(wrt, 1, 2).astype(BF16)
    br = jnp.concatenate([b_router_group, b_router_expert,
                          jnp.zeros((depth, RROWS - N_GROUPS - N_EXP), F32)], axis=-1)
    br_lanes = jnp.pad(br, ((0, 0), (0, RLANES - RROWS)))[:, None, :]
    br = br[..., None]
    caw = jnp.broadcast_to(conv_a_w[:, :, None, :], (depth, CONV_A, 8, A_W))
    ccw = jnp.pad(conv_c_w, ((0, 0), (0, 8 - CONV_C), (0, 0)))
    r3 = lambda v: v.reshape(depth, 1, -1)
    sinkc = jnp.broadcast_to((LOG2E * attn_sink).reshape(depth, N_KV, GRP, 1, 1),
                             (depth, N_KV, GRP, BLK, 1)).reshape(depth, N_KV, GRP * BLK, 1)
    bias = _attn_bias()
    tri = jnp.asarray(np.triu(np.ones((TM, TM), np.float32), 1)).astype(BF16)

    x_mid = ysort = pos = None
    for l in range(depth):
        if l == 0:
            a, q, kv, cc = _pre(l, False, (xp, xs), mod4, r3(norm_mix_g), w_in_b, T, npt, tps)
            x_in = (xp, xs)
        else:
            x, a, q, kv, cc = _pre(l, True, (x_mid, ysort), mod4, r3(norm_mix_g), w_in_b, T, npt, tps, pos)
            x_in = (x,)
        x_mid, h2r, route, counts = _mix(
            l, l == 0, x_in, a, q, kv, cc, mod4, r3(norm_ffn_g), caw, r3(conv_a_b), r3(ln_a_g),
            r3(ln_a_b), ccw, w_out_b, wrt, br, bias, sinkc, tri, T, npt, tps, tp, ss)
        pos, tile_ea, tile_eb, tile_nv = _plan(route, counts, n_pad)
        src = _invert(pos, n_pad)
        ysort = _experts(l, tile_ea, tile_eb, tile_nv, src, h2r, wr_cols, br_lanes, wgu_b, wd_b, n_pad)

    fg = final_norm_g.reshape(1, D)
    y_p = _final(depth - 1, pos, x_mid, ysort, mod4, fg, 0, npt, npt, tps)
    y_s = _final(depth - 1, pos, x_mid, ysort, mod4, fg, npt, bs * tps, npt, tps)
    return y_p.reshape(bp, sp, D), y_s.reshape(bs, ss, D)
```

```python
import functools

import jax
import jax.numpy as jnp
import numpy as np
from jax import lax
from jax.experimental import pallas as pl
from jax.experimental.pallas import tpu as pltpu

F32 = jnp.float32
BF16 = jnp.bfloat16

D = 2048
N_HEADS = 8
HEAD_DIM = 128
N_KV = 2
GRP = N_HEADS // N_KV
A_W = 512
C_W = 512
ATT_W = N_HEADS * HEAD_DIM
KV_W = N_KV * HEAD_DIM
D_IN = 4096
CONV_A = 31
CONV_C = 3
WINDOW = 128
BLK = 128
N_GROUPS = 4
EPG = 4
N_EXP = N_GROUPS * EPG
D_EXP = 512
N_PAIR = 6
N_BUCKET = N_GROUPS * N_PAIR
EPS = 1e-6
NEG = -1e30
LOG2E = 1.4426950408889634

TM = 256
TME = 256
HALO = 16
LT = D // 128
LP = LT + 1
ROUTE_ROWS = 8
RROWS = 32
VMEM_LIMIT = 56 * 1024 * 1024


def _cparams(sem=("arbitrary",)):
    return pltpu.CompilerParams(dimension_semantics=sem, vmem_limit_bytes=VMEM_LIMIT)


def _resident(shape, index_map):
    return pl.BlockSpec(shape, index_map, pipeline_mode=pl.Buffered(1))


def _ada_body(c_ref, w_ref, b_ref, o_ref):
    c = c_ref[...]
    sc = c * jax.nn.sigmoid(c)
    o_ref[...] = jnp.dot(sc, w_ref[...], preferred_element_type=F32,
                         precision=lax.Precision.HIGHEST) + b_ref[...]


def _ada(c_all, w_ada, b_ada):
    depth = w_ada.shape[0]
    tn = 1024
    return pl.pallas_call(
        _ada_body,
        grid=(depth, 6 * D // tn),
        in_specs=[pl.BlockSpec((8, D), lambda l, j: (0, 0)),
                  pl.BlockSpec((None, D, tn), lambda l, j: (l, 0, j)),
                  pl.BlockSpec((None, 1, tn), lambda l, j: (l, 0, j))],
        out_specs=pl.BlockSpec((None, 8, tn), lambda l, j: (l, 0, j)),
        out_shape=jax.ShapeDtypeStruct((depth, 8, 6 * D), F32),
        compiler_params=_cparams(("arbitrary", "arbitrary")),
    )(c_all, w_ada, b_ada.reshape(depth, 1, 6 * D))


def _seq_of_tile(i, npt, tps):
    return jnp.where(i < npt, 0, 1 + (i - npt) // tps)


def _rows_to_tile(ref, n):
    return jnp.concatenate([ref[pl.ds(c, n, stride=LP), :] for c in range(LT)], axis=1)


def _tile_to_rows(ref, val, n, extra=None):
    for c in range(LT):
        ref[pl.ds(c, n, stride=LP), :] = val[:, c * 128:(c + 1) * 128]
    ref[pl.ds(LT, n, stride=LP), :] = jnp.zeros((n, 128), F32) if extra is None else extra


def _row_dma(src, dst, s, d, sem):
    return pltpu.make_async_copy(src.at[pl.ds(s * LP, LP)], dst.at[pl.ds(d * LP, LP)], sem)


def _gather_tile(idx_ref, src_hbm, buf, sem, n, live, base=0):
    i = pl.program_id(0)
    slot = i % 2

    def start(step, s):
        @pl.when(live(step))
        def _():
            def issue(r, c):
                _row_dma(src_hbm, buf.at[s], idx_ref[base + step * n + r], r, sem.at[s]).start()
                return c
            lax.fori_loop(0, n, issue, 0, unroll=8)

    @pl.when(i == 0)
    def _():
        start(i, slot)

    @pl.when(i + 1 < pl.num_programs(0))
    def _():
        start(i + 1, 1 - slot)

    @pl.when(live(i))
    def _():
        pltpu.make_async_copy(src_hbm.at[pl.ds(0, n * LP)], buf.at[slot], sem.at[slot]).wait()

    return slot


def _rms_mod(x, g, scale, shift):
    y = x * lax.rsqrt(jnp.mean(x * x, axis=-1, keepdims=True) + EPS) * g
    return y * (1.0 + scale) + shift


def _pre_body(fused, npt, *refs):
    if fused:
        (pos_ref, xm_ref, ys_ref, modp_ref, mod_ref, g_ref, w_ref,
         x_out, a_out, q_out, kv_out, cc_out, ybuf, ysem) = refs
        slot = _gather_tile(pos_ref, ys_ref, ybuf, ysem, TM, lambda step: step >= 0)
        x = xm_ref[...] + modp_ref[:, 5 * D:6 * D] * _rows_to_tile(ybuf.at[slot], TM)
        x_out[...] = x
    else:
        xp_ref, xs_ref, mod_ref, g_ref, w_ref, a_out, q_out, kv_out, cc_out = refs
        x = jnp.where(pl.program_id(0) < npt, xp_ref[...], xs_ref[...])
    h = _rms_mod(x, g_ref[...], mod_ref[:, D:2 * D], mod_ref[:, 0:D]).astype(BF16)

    za = jnp.dot(h, w_ref[:, 0:2 * A_W], preferred_element_type=F32)
    a_out[...] = (za[:, :A_W] * jax.nn.sigmoid(za[:, A_W:])).astype(BF16)
    o = 2 * A_W
    zq = jnp.dot(h, w_ref[:, o:o + ATT_W], preferred_element_type=F32)
    q_out[...] = (zq * (LOG2E * HEAD_DIM ** -0.5)).astype(BF16)
    o += ATT_W
    kv_out[...] = jnp.dot(h, w_ref[:, o:o + 2 * KV_W], preferred_element_type=F32).astype(BF16)
    o += 2 * KV_W
    zc = jnp.dot(h, w_ref[:, o:o + 3 * C_W], preferred_element_type=F32)
    cc_out[:, 0:C_W] = (zc[:, 2 * C_W:] * zc[:, :C_W]).astype(BF16)
    cc_out[:, C_W:] = zc[:, C_W:2 * C_W].astype(BF16)


def _ring_scratch(n):
    return [pltpu.VMEM((2, n * LP, 128), F32), pltpu.SemaphoreType.DMA((2,))]


def _pre(layer, fused, xs_in, mod4, norm_g, w_in_b, T, npt, tps, pos=None):
    nt = T // TM
    seq = lambda i: _seq_of_tile(i, npt, tps)
    row = lambda i, *_: (i, 0)
    mod_spec = lambda l: pl.BlockSpec((None, None, 1, 6 * D), lambda i, *_: (l, seq(i), 0, 0))
    if fused:
        x_specs = [pl.BlockSpec((TM, D), row), pl.BlockSpec(memory_space=pl.ANY), mod_spec(layer - 1)]
    else:
        nst = xs_in[1].shape[0] // TM
        x_specs = [pl.BlockSpec((TM, D), lambda i: (jnp.minimum(i, npt - 1), 0)),
                   pl.BlockSpec((TM, D), lambda i: (jnp.clip(i - npt, 0, nst - 1), 0))]
    in_specs = x_specs + [mod_spec(layer),
                          pl.BlockSpec((None, 1, D), lambda i, *_: (layer, 0, 0)),
                          _resident((None, D, D_IN), lambda i, *_: (layer, 0, 0))]
    out_shape = [jax.ShapeDtypeStruct((T, A_W), BF16), jax.ShapeDtypeStruct((T, ATT_W), BF16),
                 jax.ShapeDtypeStruct((T, 2 * KV_W), BF16), jax.ShapeDtypeStruct((T, 2 * C_W), BF16)]
    out_specs = [pl.BlockSpec((TM, A_W), row), pl.BlockSpec((TM, ATT_W), row),
                 pl.BlockSpec((TM, 2 * KV_W), row), pl.BlockSpec((TM, 2 * C_W), row)]
    if fused:
        out_shape = [jax.ShapeDtypeStruct((T, D), F32)] + out_shape
        out_specs = [pl.BlockSpec((TM, D), row)] + out_specs
    args = ([pos] if fused else []) + list(xs_in) + ([mod4] if fused else []) + [mod4, norm_g, w_in_b]
    return pl.pallas_call(
        functools.partial(_pre_body, fused, npt),
        grid_spec=pltpu.PrefetchScalarGridSpec(
            num_scalar_prefetch=1 if fused else 0, grid=(nt,), in_specs=in_specs, out_specs=out_specs,
            scratch_shapes=_ring_scratch(TM) if fused else []),
        out_shape=out_shape, compiler_params=_cparams(),
    )(*args)


def _route_rows(lt):
    lg = [lt[r:r + 1, :] for r in range(N_GROUPS)]
    mg = functools.reduce(jnp.maximum, lg)
    gp = 1.0 / functools.reduce(jnp.add, [jnp.exp(l - mg) for l in lg])
    g = jnp.where(lg[0] == mg, 0.0, jnp.where(lg[1] == mg, 1.0, jnp.where(lg[2] == mg, 2.0, 3.0)))
    le = [lt[N_GROUPS + e:N_GROUPS + e + 1, :] for e in range(N_EXP)]
    sel = [jnp.where(g == 0.0, le[j], jnp.where(g == 1.0, le[EPG + j],
                                                jnp.where(g == 2.0, le[2 * EPG + j], le[3 * EPG + j])))
           for j in range(EPG)]
    ms = functools.reduce(jnp.maximum, sel)
    es = [jnp.exp(s - ms) for s in sel]
    zs = functools.reduce(jnp.add, es)
    p = [e / zs for e in es]
    v1 = functools.reduce(jnp.maximum, p)
    i1 = jnp.where(p[0] == v1, 0.0, jnp.where(p[1] == v1, 1.0, jnp.where(p[2] == v1, 2.0, 3.0)))
    p2 = [jnp.where(i1 == float(j), -1.0, p[j]) for j in range(EPG)]
    v2 = functools.reduce(jnp.maximum, p2)
    i2 = jnp.where(p2[0] == v2, 0.0, jnp.where(p2[1] == v2, 1.0, jnp.where(p2[2] == v2, 2.0, 3.0)))
    den = v1 + v2
    first = i1 < i2
    a = jnp.where(first, i1, i2)
    b = jnp.where(first, i2, i1)
    w_a = gp * (jnp.where(first, v1, v2) / den)
    w_b = gp * (jnp.where(first, v2, v1) / den)
    pair = jnp.where(a == 0.0, b - 1.0, jnp.where(a == 1.0, b + 1.0, 5.0))
    return g * float(N_PAIR) + pair, w_a, w_b


def _mix_body(layer0, npt, tp, ss, *refs):
    if layer0:
        xp_ref, xs_ref = refs[0], refs[1]
        refs = refs[2:]
    else:
        x_ref = refs[0]
        refs = refs[1:]
    (a_cur, a_prev, a_next, q_ref, kv_cur, kv_prev, kv_next, cc_cur, cc_prev, cc_next,
     mod_ref, gf_ref, caw_ref, cab_ref, lng_ref, lnb_ref, ccw_ref, wout_ref, wrt_ref, br_ref,
     bias_ref, sink_ref, tri_ref,
     xmid_out, h2r_out, route_out, cnt_out,
     abuf, cbuf, kvbuf, mixed, carry) = refs

    i = pl.program_id(0)
    t0 = i * TM
    in_p = t0 < tp
    p0 = jnp.where(in_p, t0, (t0 - tp) % ss)
    slen = jnp.where(in_p, tp, ss)
    is_first = p0 == 0
    is_last = p0 + TM == slen

    @pl.when(i == 0)
    def _():
        carry[...] = jnp.zeros_like(carry)

    abuf[0, 0:HALO, :] = jnp.where(is_first, 0.0, a_prev[...].astype(F32))
    abuf[0, HALO:HALO + TM, :] = a_cur[...].astype(F32)
    abuf[0, HALO + TM:2 * HALO + TM, :] = jnp.where(is_last, 0.0, a_next[...].astype(F32))
    for k in range(1, 8):
        abuf[k, 0:TM + 24, :] = abuf[0, k:k + TM + 24, :]
    rc = 64

    def conv_a_chunk(r):
        acc = jnp.zeros((rc // 8, 8, A_W), F32)
        for j in range(CONV_A):
            s = HALO - (CONV_A - 1) // 2 + j
            o = r * rc + s // 8 * 8
            acc = acc + caw_ref[j] * abuf[s % 8, o:o + rc, :].reshape(rc // 8, 8, A_W)
        acc = acc.reshape(rc, A_W) + cab_ref[...]
        mu = jnp.mean(acc, axis=-1, keepdims=True)
        xc = acc - mu
        var = jnp.mean(xc * xc, axis=-1, keepdims=True)
        yn = xc * lax.rsqrt(var + EPS) * lng_ref[...] + lnb_ref[...]
        mixed[r * rc:(r + 1) * rc, 0:A_W] = (yn * jax.nn.sigmoid(yn)).astype(BF16)

    cbuf[0:HALO, :] = jnp.where(is_first, 0.0, cc_prev[:, 0:C_W].astype(F32))
    cbuf[HALO:HALO + TM, :] = cc_cur[:, 0:C_W].astype(F32)
    cbuf[HALO + TM:2 * HALO + TM, :] = jnp.where(is_last, 0.0, cc_next[:, 0:C_W].astype(F32))

    def conv_c_chunk(r):
        acc = jnp.zeros((rc, C_W), F32)
        for j in range(CONV_C):
            s = r * rc + HALO - (CONV_C - 1) // 2 + j
            acc = acc + ccw_ref[j:j + 1, :] * cbuf[s:s + rc, :]
        gate = cc_cur[r * rc:(r + 1) * rc, C_W:2 * C_W].astype(F32)
        mixed[r * rc:(r + 1) * rc, A_W + ATT_W:] = (gate * acc).astype(BF16)

    kvbuf[0:BLK, :] = kv_prev[...]
    kvbuf[BLK:BLK + TM, :] = kv_cur[...]
    kvbuf[BLK + TM:2 * BLK + TM, :] = kv_next[...]
    col = lax.broadcasted_iota(jnp.int32, (1, 3 * BLK), 1)
    pen_first = jnp.where(jnp.logical_and(is_first, col < BLK), NEG, 0.0)
    pen_last = jnp.where(jnp.logical_and(is_last, col >= 2 * BLK), NEG, 0.0)
    nqb = TM // BLK

    def attn_block(qb, kh):
        qs = jnp.concatenate(
            [q_ref[qb * BLK:(qb + 1) * BLK, (kh * GRP + g) * HEAD_DIM:(kh * GRP + g + 1) * HEAD_DIM]
             for g in range(GRP)], axis=0)
        kb = kvbuf[qb * BLK:qb * BLK + 3 * BLK, kh * HEAD_DIM:(kh + 1) * HEAD_DIM]
        vb = kvbuf[qb * BLK:qb * BLK + 3 * BLK, KV_W + kh * HEAD_DIM:KV_W + (kh + 1) * HEAD_DIM]
        s = lax.dot_general(qs, kb, (((1,), (1,)), ((), ())), preferred_element_type=F32)
        s = s + bias_ref[kh]
        if qb == 0:
            s = s + pen_first
        if qb == nqb - 1:
            s = s + pen_last
        sk = sink_ref[kh]
        m = jnp.maximum(jnp.max(s, axis=-1, keepdims=True), sk)
        p = jnp.exp2(s - m)
        den = jnp.sum(p, axis=-1, keepdims=True) + jnp.exp2(sk - m)
        o = jnp.dot(p.astype(BF16), vb, preferred_element_type=F32) / den
        for g in range(GRP):
            c0 = A_W + (kh * GRP + g) * HEAD_DIM
            mixed[qb * BLK:(qb + 1) * BLK, c0:c0 + HEAD_DIM] = o[g * BLK:(g + 1) * BLK].astype(BF16)

    blocks = [(qb, kh) for qb in range(nqb) for kh in range(N_KV)]
    for r in range(TM // rc):
        attn_block(*blocks[r])
        conv_a_chunk(r)
    for k in range(TM // rc, len(blocks)):
        attn_block(*blocks[k])
    for r in range(TM // rc):
        conv_c_chunk(r)

    mix = jnp.dot(mixed[...], wout_ref[...], preferred_element_type=F32)
    if layer0:
        x = jnp.where(i < npt, xp_ref[...], xs_ref[...])
    else:
        x = x_ref[...]
    xmid = x + mod_ref[:, 2 * D:3 * D] * mix
    xmid_out[...] = xmid
    h2 = _rms_mod(xmid, gf_ref[...], mod_ref[:, 4 * D:5 * D], mod_ref[:, 3 * D:4 * D])

    lt = lax.dot_general(wrt_ref[...], h2.astype(BF16), (((1,), (1,)), ((), ())),
                         preferred_element_type=F32) + br_ref[...]
    bucket, w_a, w_b = _route_rows(lt)
    pay = jnp.concatenate([w_a, w_b, jnp.zeros((128 - 2, TM), F32)], axis=0).T
    _tile_to_rows(h2r_out, h2, TM, extra=pay)
    rows = lax.broadcasted_iota(jnp.int32, (RROWS, TM), 0).astype(F32)
    oh = jnp.where(rows == bucket, 1.0, 0.0)
    before = jnp.dot(oh.astype(BF16), tri_ref[...], preferred_element_type=F32)
    rank = jnp.sum(oh * (before + carry[:, 0:1]), axis=0, keepdims=True)
    carry[...] = carry[...] + jnp.sum(oh, axis=1, keepdims=True)
    cnt_out[...] = carry[...]
    zrow = jnp.zeros((ROUTE_ROWS - 2, TM), F32)
    route_out[...] = jnp.concatenate([bucket, rank, zrow], axis=0)


def _mix(layer, layer0, x_in, a, q, kv, cc, mod4, norm_ffn_g, caw, cab, lng, lnb, ccw, w_out_b,
         wrt, br, bias, sinkc, tri, T, npt, tps, tp, ss):
    nt = T // TM
    seq = lambda i: _seq_of_tile(i, npt, tps)
    row = lambda i: (i, 0)
    hp = TM // HALO
    bp = TM // BLK
    prev = lambda per: (lambda i: (jnp.maximum(i * per - 1, 0), 0))
    nxt = lambda per, n: (lambda i: (jnp.minimum((i + 1) * per, n - 1), 0))
    const2 = lambda i: (0, 0)
    lsel = lambda i: (layer, 0, 0)
    if layer0:
        nst = x_in[1].shape[0] // TM
        x_specs = [pl.BlockSpec((TM, D), lambda i: (jnp.minimum(i, npt - 1), 0)),
                   pl.BlockSpec((TM, D), lambda i: (jnp.clip(i - npt, 0, nst - 1), 0))]
    else:
        x_specs = [pl.BlockSpec((TM, D), row)]
    in_specs = x_specs + [
        pl.BlockSpec((TM, A_W), row),
        pl.BlockSpec((HALO, A_W), prev(hp)), pl.BlockSpec((HALO, A_W), nxt(hp, T // HALO)),
        pl.BlockSpec((TM, ATT_W), row),
        pl.BlockSpec((TM, 2 * KV_W), row),
        pl.BlockSpec((BLK, 2 * KV_W), prev(bp)), pl.BlockSpec((BLK, 2 * KV_W), nxt(bp, T // BLK)),
        pl.BlockSpec((TM, 2 * C_W), row),
        pl.BlockSpec((HALO, 2 * C_W), prev(hp)), pl.BlockSpec((HALO, 2 * C_W), nxt(hp, T // HALO)),
        pl.BlockSpec((None, None, 1, 6 * D), lambda i: (layer, seq(i), 0, 0)),
        pl.BlockSpec((None, 1, D), lsel),
        pl.BlockSpec((None, CONV_A, 8, A_W), lambda i: (layer, 0, 0, 0)),
        pl.BlockSpec((None, 1, A_W), lsel), pl.BlockSpec((None, 1, A_W), lsel),
        pl.BlockSpec((None, 1, A_W), lsel),
        pl.BlockSpec((None, 8, C_W), lsel),
        _resident((None, D, D), lsel),
        pl.BlockSpec((None, RROWS, D), lsel),
        pl.BlockSpec((None, RROWS, 1), lsel),
        pl.BlockSpec((N_KV, GRP * BLK, 3 * BLK), lambda i: (0, 0, 0)),
        pl.BlockSpec((None, N_KV, GRP * BLK, 1), lambda i: (layer, 0, 0, 0)),
        pl.BlockSpec((TM, TM), const2),
    ]
    out_shape = [jax.ShapeDtypeStruct((T, D), F32), jax.ShapeDtypeStruct((T * LP, 128), F32),
                 jax.ShapeDtypeStruct((ROUTE_ROWS, T), F32), jax.ShapeDtypeStruct((RROWS, 128), F32)]
    out_specs = [pl.BlockSpec((TM, D), row), pl.BlockSpec((TM * LP, 128), row),
                 pl.BlockSpec((ROUTE_ROWS, TM), lambda i: (0, i)), pl.BlockSpec((RROWS, 128), const2)]
    scratch = [pltpu.VMEM((8, TM + 2 * HALO, A_W), F32), pltpu.VMEM((TM + 2 * HALO, C_W), F32),
               pltpu.VMEM((TM + 2 * BLK, 2 * KV_W), BF16), pltpu.VMEM((TM, D), BF16),
               pltpu.VMEM((RROWS, 128), F32)]
    args = list(x_in) + [a, a, a, q, kv, kv, kv, cc, cc, cc, mod4, norm_ffn_g, caw, cab, lng, lnb, ccw,
                         w_out_b, wrt, br, bias, sinkc, tri]
    return pl.pallas_call(
        functools.partial(_mix_body, layer0, npt, tp, ss),
        grid=(nt,), in_specs=in_specs, out_specs=out_specs, out_shape=out_shape,
        scratch_shapes=scratch, compiler_params=_cparams(),
    )(*args)


def _invert_body(pos_ref, lo_ref, hi_ref, src_ref):
    def pad_bucket(b, c):
        def zero(k, c2):
            src_ref[k] = 0
            return c2
        return lax.fori_loop(lo_ref[b], hi_ref[b], zero, c)

    lax.fori_loop(0, lo_ref.shape[0], pad_bucket, 0)

    def put(t, c):
        src_ref[pos_ref[t]] = t
        return c

    lax.fori_loop(0, pos_ref.shape[0], put, 0, unroll=32)


def _invert(pos, pad_lo, pad_hi, n_pad):
    return pl.pallas_call(
        _invert_body,
        grid_spec=pltpu.PrefetchScalarGridSpec(
            num_scalar_prefetch=3, grid=(1,), in_specs=[],
            out_specs=pl.BlockSpec(memory_space=pltpu.SMEM)),
        out_shape=jax.ShapeDtypeStruct((n_pad,), jnp.int32),
    )(pos, pad_lo, pad_hi)


def _expert_half(x, wg_ref, wu_ref, wd_ref):
    gate = jnp.dot(x, wg_ref[...], preferred_element_type=F32)
    he = gate * jax.nn.sigmoid(gate) * jnp.dot(x, wu_ref[...], preferred_element_type=F32)
    return jnp.dot(he.astype(BF16), wd_ref[...], preferred_element_type=F32)


def _expert_body(ea_ref, eb_ref, nv_ref, src_ref, h_hbm, wg_a, wu_a, wd_a, wg_b, wu_b, wd_b, y_ref, xbuf, xsem):
    j = pl.program_id(0)
    nv = nv_ref[j]
    slot = _gather_tile(src_ref, h_hbm, xbuf, xsem, TME, lambda step: nv_ref[step] > 0)

    @pl.when(nv > 0)
    def _():
        rows = xbuf.at[slot]
        x = _rows_to_tile(rows, TME).astype(BF16)
        w = rows[pl.ds(LT, TME, stride=LP), :]
        y = w[:, 0:1] * _expert_half(x, wg_a, wu_a, wd_a) + w[:, 1:2] * _expert_half(x, wg_b, wu_b, wd_b)
        _tile_to_rows(y_ref, y, TME)

    @pl.when(nv == 0)
    def _():
        y_ref[...] = jnp.zeros_like(y_ref)


def _experts(layer, tile_ea, tile_eb, tile_nv, src, h2r, wg_b, wu_b, wd_b, n_pad):
    nt = n_pad // TME
    wsel = lambda ref_idx: (lambda j, ea, eb, nv, src: (layer, (ea, eb)[ref_idx][j], 0, 0))
    up = lambda k: pl.BlockSpec((None, None, D, D_EXP), wsel(k))
    down = lambda k: pl.BlockSpec((None, None, D_EXP, D), wsel(k))
    return pl.pallas_call(
        _expert_body,
        grid_spec=pltpu.PrefetchScalarGridSpec(
            num_scalar_prefetch=4, grid=(nt,),
            in_specs=[pl.BlockSpec(memory_space=pl.ANY), up(0), up(0), down(0), up(1), up(1), down(1)],
            out_specs=pl.BlockSpec((TME * LP, 128), lambda j, ea, eb, nv, src: (j, 0)),
            scratch_shapes=_ring_scratch(TME)),
        out_shape=jax.ShapeDtypeStruct((n_pad * LP, 128), F32),
        compiler_params=_cparams(),
    )(tile_ea, tile_eb, tile_nv, src, h2r, wg_b, wu_b, wd_b, wg_b, wu_b, wd_b)


def _final_body(tile0, pos_ref, xm_ref, ys_ref, mod_ref, g_ref, o_ref, ybuf, ysem):
    slot = _gather_tile(pos_ref, ys_ref, ybuf, ysem, TM, lambda step: step >= 0, base=tile0 * TM)
    x = xm_ref[...] + mod_ref[:, 5 * D:6 * D] * _rows_to_tile(ybuf.at[slot], TM)
    o_ref[...] = x * lax.rsqrt(jnp.mean(x * x, axis=-1, keepdims=True) + EPS) * g_ref[...]


def _final(layer, pos, xmid, ysort, mod4, final_g, tile0, ntiles, npt, tps):
    seq = lambda i: _seq_of_tile(i + tile0, npt, tps)
    return pl.pallas_call(
        functools.partial(_final_body, tile0),
        grid_spec=pltpu.PrefetchScalarGridSpec(
            num_scalar_prefetch=1, grid=(ntiles,),
            in_specs=[pl.BlockSpec((TM, D), lambda i, pos: (i + tile0, 0)),
                      pl.BlockSpec(memory_space=pl.ANY),
                      pl.BlockSpec((None, None, 1, 6 * D), lambda i, pos: (layer, seq(i), 0, 0)),
                      pl.BlockSpec((1, D), lambda i, pos: (0, 0))],
            out_specs=pl.BlockSpec((TM, D), lambda i, pos: (i, 0)),
            scratch_shapes=_ring_scratch(TM)),
        out_shape=jax.ShapeDtypeStruct((ntiles * TM, D), F32),
        compiler_params=_cparams(),
    )(pos, xmid, ysort, mod4, final_g)


def _bucket_tables():
    ea, eb = [], []
    for g in range(N_GROUPS):
        for a in range(EPG):
            for b in range(a + 1, EPG):
                ea.append(g * EPG + a)
                eb.append(g * EPG + b)
    return np.asarray(ea, np.int32), np.asarray(eb, np.int32)


def _plan(route, counts, n_pad):
    bucket = route[0].astype(jnp.int32)
    rank = route[1].astype(jnp.int32)
    cnt = counts[:N_BUCKET, 0].astype(jnp.int32)
    padded = (cnt + TME - 1) // TME * TME
    ends = jnp.cumsum(padded)
    starts = ends - padded
    pos = starts[bucket] + rank
    nt = n_pad // TME
    tile_start = jnp.arange(nt, dtype=jnp.int32) * TME
    n_used = ends[-1] // TME
    tb = jnp.sum(tile_start[:, None] >= ends[None, :], axis=1).astype(jnp.int32)
    last_b = jnp.max(jnp.where(cnt > 0, jnp.arange(N_BUCKET, dtype=jnp.int32), 0))
    used = jnp.arange(nt, dtype=jnp.int32) < n_used
    tb = jnp.where(used, jnp.minimum(tb, N_BUCKET - 1), last_b)
    ea_tab, eb_tab = _bucket_tables()
    tile_ea = jnp.asarray(ea_tab)[tb]
    tile_eb = jnp.asarray(eb_tab)[tb]
    tile_nv = jnp.where(used, jnp.clip(starts[tb] + cnt[tb] - tile_start, 0, TME), 0).astype(jnp.int32)
    pad_lo = jnp.concatenate([starts + cnt, ends[-1:]]).astype(jnp.int32)
    pad_hi = jnp.concatenate([ends, jnp.full((1,), n_pad, jnp.int32)]).astype(jnp.int32)
    return pos, tile_ea, tile_eb, tile_nv, pad_lo, pad_hi


def _attn_bias():
    r = np.arange(BLK)[:, None]
    c = np.arange(3 * BLK)[None, :]
    dist = np.abs(c - BLK - r).astype(np.float32)
    slopes = 2.0 ** (-8.0 * np.arange(1, N_HEADS + 1, dtype=np.float32) / N_HEADS)
    tab = np.where(dist[None] <= WINDOW, -LOG2E * slopes[:, None, None] * dist[None], NEG).astype(np.float32)
    return jnp.asarray(tab.reshape(N_KV, GRP * BLK, 3 * BLK))


def kernel(x_prompt, x_sample, c_prompt, c_sample, norm_mix_g, norm_ffn_g, w_ada, b_ada, w_in, w_out,
           conv_a_w, conv_a_b, ln_a_g, ln_a_b, attn_sink, conv_c_w, w_router_group, b_router_group,
           w_router_expert, b_router_expert, w_gate, w_up, w_down, final_norm_g):
    depth = w_in.shape[0]
    bp, sp, _ = x_prompt.shape
    bs, ss, _ = x_sample.shape
    assert bp == 1 and sp % TM == 0 and ss % TM == 0 and (bp + bs) <= 8
    tp = bp * sp
    T = tp + bs * ss
    npt, tps = tp // TM, ss // TM
    n_pad = T + N_BUCKET * TME

    xp = x_prompt.reshape(tp, D)
    xs = x_sample.reshape(bs * ss, D)
    c_all = jnp.concatenate([c_prompt, c_sample, jnp.zeros((8 - bp - bs, D), F32)], axis=0)
    mod4 = _ada(c_all, w_ada, b_ada).reshape(depth, 8, 1, 6 * D)

    w_in_b = w_in.astype(BF16)
    w_out_b = w_out.astype(BF16)
    wg_b = w_gate.astype(BF16)
    wu_b = w_up.astype(BF16)
    wd_b = w_down.astype(BF16)
    wrt = jnp.concatenate([w_router_group, w_router_expert,
                           jnp.zeros((depth, D, RROWS - N_GROUPS - N_EXP), F32)], axis=-1)
    wrt = jnp.swapaxes(wrt, 1, 2).astype(BF16)
    br = jnp.concatenate([b_router_group, b_router_expert,
                          jnp.zeros((depth, RROWS - N_GROUPS - N_EXP), F32)], axis=-1)[..., None]
    caw = jnp.broadcast_to(conv_a_w[:, :, None, :], (depth, CONV_A, 8, A_W))
    ccw = jnp.pad(conv_c_w, ((0, 0), (0, 8 - CONV_C), (0, 0)))
    r3 = lambda v: v.reshape(depth, 1, -1)
    sinkc = jnp.broadcast_to((LOG2E * attn_sink).reshape(depth, N_KV, GRP, 1, 1),
                             (depth, N_KV, GRP, BLK, 1)).reshape(depth, N_KV, GRP * BLK, 1)
    bias = _attn_bias()
    tri = jnp.asarray(np.triu(np.ones((TM, TM), np.float32), 1)).astype(BF16)

    x_mid = ysort = pos = None
    for l in range(depth):
        if l == 0:
            a, q, kv, cc = _pre(l, False, (xp, xs), mod4, r3(norm_mix_g), w_in_b, T, npt, tps)
            x_in = (xp, xs)
        else:
            x, a, q, kv, cc = _pre(l, True, (x_mid, ysort), mod4, r3(norm_mix_g), w_in_b, T, npt, tps, pos)
            x_in = (x,)
        x_mid, h2r, route, counts = _mix(
            l, l == 0, x_in, a, q, kv, cc, mod4, r3(norm_ffn_g), caw, r3(conv_a_b), r3(ln_a_g),
            r3(ln_a_b), ccw, w_out_b, wrt, br, bias, sinkc, tri, T, npt, tps, tp, ss)
        pos, tile_ea, tile_eb, tile_nv, pad_lo, pad_hi = _plan(route, counts, n_pad)
        src = _invert(pos, pad_lo, pad_hi, n_pad)
        ysort = _experts(l, tile_ea, tile_eb, tile_nv, src, h2r, wg_b, wu_b, wd_b, n_pad)

    fg = final_norm_g.reshape(1, D)
    y_p = _final(depth - 1, pos, x_mid, ysort, mod4, fg, 0, npt, npt, tps)
    y_s = _final(depth - 1, pos, x_mid, ysort, mod4, fg, npt, bs * tps, npt, tps)
    return y_p.reshape(bp, sp, D), y_s.reshape(bs, ss, D)
```

```python
import functools

import jax
import jax.numpy as jnp
import numpy as np
from jax import lax
from jax.experimental import pallas as pl
from jax.experimental.pallas import tpu as pltpu

F32 = jnp.float32
BF16 = jnp.bfloat16

D = 2048
N_HEADS = 8
HEAD_DIM = 128
N_KV = 2
GRP = N_HEADS // N_KV
A_W = 512
C_W = 512
ATT_W = N_HEADS * HEAD_DIM
KV_W = N_KV * HEAD_DIM
D_IN = 4096
CONV_A = 31
CONV_C = 3
WINDOW = 128
BLK = 128
N_GROUPS = 4
EPG = 4
N_EXP = N_GROUPS * EPG
D_EXP = 512
N_PAIR = 6
N_BUCKET = N_GROUPS * N_PAIR
EPS = 1e-6
NEG = -1e30
LOG2E = 1.4426950408889634

TM = 256
TME = 256
HALO = 16
LT = D // 128
LP = LT + 1
ROUTE_ROWS = 8
RROWS = 32
VMEM_LIMIT = 56 * 1024 * 1024


def _cparams(sem=("arbitrary",)):
    return pltpu.CompilerParams(dimension_semantics=sem, vmem_limit_bytes=VMEM_LIMIT)


def _resident(shape, index_map):
    return pl.BlockSpec(shape, index_map, pipeline_mode=pl.Buffered(1))


def _ada_body(c_ref, w_ref, b_ref, o_ref):
    c = c_ref[...]
    sc = c * jax.nn.sigmoid(c)
    o_ref[...] = jnp.dot(sc, w_ref[...], preferred_element_type=F32,
                         precision=lax.Precision.HIGHEST) + b_ref[...]


def _ada(c_all, w_ada, b_ada):
    depth = w_ada.shape[0]
    tn = 2048
    return pl.pallas_call(
        _ada_body,
        grid=(depth, 6 * D // tn),
        in_specs=[pl.BlockSpec((8, D), lambda l, j: (0, 0)),
                  pl.BlockSpec((None, D, tn), lambda l, j: (l, 0, j)),
                  pl.BlockSpec((None, 1, tn), lambda l, j: (l, 0, j))],
        out_specs=pl.BlockSpec((None, 8, tn), lambda l, j: (l, 0, j)),
        out_shape=jax.ShapeDtypeStruct((depth, 8, 6 * D), F32),
        compiler_params=_cparams(("arbitrary", "arbitrary")),
    )(c_all, w_ada, b_ada.reshape(depth, 1, 6 * D))


def _seq_of_tile(i, npt, tps):
    return jnp.where(i < npt, 0, 1 + (i - npt) // tps)


def _rows_to_tile(ref, n):
    return jnp.concatenate([ref[pl.ds(c, n, stride=LP), :] for c in range(LT)], axis=1)


def _tile_to_rows(ref, val, n, extra=None):
    for c in range(LT):
        ref[pl.ds(c, n, stride=LP), :] = val[:, c * 128:(c + 1) * 128]
    ref[pl.ds(LT, n, stride=LP), :] = jnp.zeros((n, 128), F32) if extra is None else extra


def _row_dma(src, dst, s, d, sem):
    return pltpu.make_async_copy(src.at[pl.ds(s * LP, LP)], dst.at[pl.ds(d * LP, LP)], sem)


def _gather_tile(idx_ref, src_hbm, buf, sem, n, live, base=0):
    i = pl.program_id(0)
    slot = i % 2

    def start(step, s):
        @pl.when(live(step))
        def _():
            def issue(r, c):
                _row_dma(src_hbm, buf.at[s], idx_ref[base + step * n + r], r, sem.at[s]).start()
                return c
            lax.fori_loop(0, n, issue, 0, unroll=8)

    @pl.when(i == 0)
    def _():
        start(i, slot)

    @pl.when(i + 1 < pl.num_programs(0))
    def _():
        start(i + 1, 1 - slot)

    @pl.when(live(i))
    def _():
        pltpu.make_async_copy(src_hbm.at[pl.ds(0, n * LP)], buf.at[slot], sem.at[slot]).wait()

    return slot


def _rms_mod(x, g, scale, shift):
    y = x * lax.rsqrt(jnp.mean(x * x, axis=-1, keepdims=True) + EPS) * g
    return y * (1.0 + scale) + shift


def _pre_body(fused, npt, *refs):
    if fused:
        (pos_ref, xm_ref, ys_ref, modp_ref, mod_ref, g_ref, w_ref,
         x_out, a_out, q_out, kv_out, cc_out, ybuf, ysem) = refs
        slot = _gather_tile(pos_ref, ys_ref, ybuf, ysem, TM, lambda step: step >= 0)
        x = xm_ref[...] + modp_ref[:, 5 * D:6 * D] * _rows_to_tile(ybuf.at[slot], TM)
        x_out[...] = x
    else:
        xp_ref, xs_ref, mod_ref, g_ref, w_ref, a_out, q_out, kv_out, cc_out = refs
        x = jnp.where(pl.program_id(0) < npt, xp_ref[...], xs_ref[...])
    h = _rms_mod(x, g_ref[...], mod_ref[:, D:2 * D], mod_ref[:, 0:D]).astype(BF16)

    za = jnp.dot(h, w_ref[:, 0:2 * A_W], preferred_element_type=F32)
    a_out[...] = (za[:, :A_W] * jax.nn.sigmoid(za[:, A_W:])).astype(BF16)
    o = 2 * A_W
    zq = jnp.dot(h, w_ref[:, o:o + ATT_W], preferred_element_type=F32)
    q_out[...] = (zq * (LOG2E * HEAD_DIM ** -0.5)).astype(BF16)
    o += ATT_W
    kv_out[...] = jnp.dot(h, w_ref[:, o:o + 2 * KV_W], preferred_element_type=F32).astype(BF16)
    o += 2 * KV_W
    zc = jnp.dot(h, w_ref[:, o:o + 3 * C_W], preferred_element_type=F32)
    cc_out[:, 0:C_W] = (zc[:, 2 * C_W:] * zc[:, :C_W]).astype(BF16)
    cc_out[:, C_W:] = zc[:, C_W:2 * C_W].astype(BF16)


def _ring_scratch(n):
    return [pltpu.VMEM((2, n * LP, 128), F32), pltpu.SemaphoreType.DMA((2,))]


def _pre(layer, fused, xs_in, mod4, norm_g, w_in_b, T, npt, tps, pos=None):
    nt = T // TM
    seq = lambda i: _seq_of_tile(i, npt, tps)
    row = lambda i, *_: (i, 0)
    mod_spec = lambda l: pl.BlockSpec((None, None, 1, 6 * D), lambda i, *_: (l, seq(i), 0, 0))
    if fused:
        x_specs = [pl.BlockSpec((TM, D), row), pl.BlockSpec(memory_space=pl.ANY), mod_spec(layer - 1)]
    else:
        nst = xs_in[1].shape[0] // TM
        x_specs = [pl.BlockSpec((TM, D), lambda i: (jnp.minimum(i, npt - 1), 0)),
                   pl.BlockSpec((TM, D), lambda i: (jnp.clip(i - npt, 0, nst - 1), 0))]
    in_specs = x_specs + [mod_spec(layer),
                          pl.BlockSpec((None, 1, D), lambda i, *_: (layer, 0, 0)),
                          _resident((None, D, D_IN), lambda i, *_: (layer, 0, 0))]
    out_shape = [jax.ShapeDtypeStruct((T, A_W), BF16), jax.ShapeDtypeStruct((T, ATT_W), BF16),
                 jax.ShapeDtypeStruct((T, 2 * KV_W), BF16), jax.ShapeDtypeStruct((T, 2 * C_W), BF16)]
    out_specs = [pl.BlockSpec((TM, A_W), row), pl.BlockSpec((TM, ATT_W), row),
                 pl.BlockSpec((TM, 2 * KV_W), row), pl.BlockSpec((TM, 2 * C_W), row)]
    if fused:
        out_shape = [jax.ShapeDtypeStruct((T, D), F32)] + out_shape
        out_specs = [pl.BlockSpec((TM, D), row)] + out_specs
    args = ([pos] if fused else []) + list(xs_in) + ([mod4] if fused else []) + [mod4, norm_g, w_in_b]
    return pl.pallas_call(
        functools.partial(_pre_body, fused, npt),
        grid_spec=pltpu.PrefetchScalarGridSpec(
            num_scalar_prefetch=1 if fused else 0, grid=(nt,), in_specs=in_specs, out_specs=out_specs,
            scratch_shapes=_ring_scratch(TM) if fused else []),
        out_shape=out_shape, compiler_params=_cparams(),
    )(*args)


def _route_rows(lt):
    lg = [lt[r:r + 1, :] for r in range(N_GROUPS)]
    mg = functools.reduce(jnp.maximum, lg)
    gp = 1.0 / functools.reduce(jnp.add, [jnp.exp(l - mg) for l in lg])
    g = jnp.where(lg[0] == mg, 0.0, jnp.where(lg[1] == mg, 1.0, jnp.where(lg[2] == mg, 2.0, 3.0)))
    le = [lt[N_GROUPS + e:N_GROUPS + e + 1, :] for e in range(N_EXP)]
    sel = [jnp.where(g == 0.0, le[j], jnp.where(g == 1.0, le[EPG + j],
                                                jnp.where(g == 2.0, le[2 * EPG + j], le[3 * EPG + j])))
           for j in range(EPG)]
    ms = functools.reduce(jnp.maximum, sel)
    es = [jnp.exp(s - ms) for s in sel]
    zs = functools.reduce(jnp.add, es)
    p = [e / zs for e in es]
    v1 = functools.reduce(jnp.maximum, p)
    i1 = jnp.where(p[0] == v1, 0.0, jnp.where(p[1] == v1, 1.0, jnp.where(p[2] == v1, 2.0, 3.0)))
    p2 = [jnp.where(i1 == float(j), -1.0, p[j]) for j in range(EPG)]
    v2 = functools.reduce(jnp.maximum, p2)
    i2 = jnp.where(p2[0] == v2, 0.0, jnp.where(p2[1] == v2, 1.0, jnp.where(p2[2] == v2, 2.0, 3.0)))
    den = v1 + v2
    first = i1 < i2
    a = jnp.where(first, i1, i2)
    b = jnp.where(first, i2, i1)
    w_a = gp * (jnp.where(first, v1, v2) / den)
    w_b = gp * (jnp.where(first, v2, v1) / den)
    pair = jnp.where(a == 0.0, b - 1.0, jnp.where(a == 1.0, b + 1.0, 5.0))
    return g * float(N_PAIR) + pair, w_a, w_b


def _mix_body(layer0, npt, tp, ss, *refs):
    if layer0:
        xp_ref, xs_ref = refs[0], refs[1]
        refs = refs[2:]
    else:
        x_ref = refs[0]
        refs = refs[1:]
    (a_cur, a_prev, a_next, q_ref, kv_cur, kv_prev, kv_next, cc_cur, cc_prev, cc_next,
     mod_ref, gf_ref, caw_ref, cab_ref, lng_ref, lnb_ref, ccw_ref, wout_ref, wrt_ref, br_ref,
     bias_ref, sink_ref, tri_ref,
     xmid_out, h2r_out, route_out, cnt_out,
     abuf, cbuf, kvbuf, mixed, carry) = refs

    i = pl.program_id(0)
    t0 = i * TM
    in_p = t0 < tp
    p0 = jnp.where(in_p, t0, (t0 - tp) % ss)
    slen = jnp.where(in_p, tp, ss)
    is_first = p0 == 0
    is_last = p0 + TM == slen

    @pl.when(i == 0)
    def _():
        carry[...] = jnp.zeros_like(carry)

    abuf[0, 0:HALO, :] = jnp.where(is_first, 0.0, a_prev[...].astype(F32))
    abuf[0, HALO:HALO + TM, :] = a_cur[...].astype(F32)
    abuf[0, HALO + TM:2 * HALO + TM, :] = jnp.where(is_last, 0.0, a_next[...].astype(F32))
    for k in range(1, 8):
        abuf[k, 0:TM + 24, :] = abuf[0, k:k + TM + 24, :]
    rc = 32

    def conv_a_chunk(r):
        acc = jnp.zeros((rc // 8, 8, A_W), F32)
        for j in range(CONV_A):
            s = HALO - (CONV_A - 1) // 2 + j
            o = r * rc + s // 8 * 8
            acc = acc + caw_ref[j] * abuf[s % 8, o:o + rc, :].reshape(rc // 8, 8, A_W)
        acc = acc.reshape(rc, A_W) + cab_ref[...]
        mu = jnp.mean(acc, axis=-1, keepdims=True)
        xc = acc - mu
        var = jnp.mean(xc * xc, axis=-1, keepdims=True)
        yn = xc * lax.rsqrt(var + EPS) * lng_ref[...] + lnb_ref[...]
        mixed[r * rc:(r + 1) * rc, 0:A_W] = (yn * jax.nn.sigmoid(yn)).astype(BF16)

    cbuf[0:HALO, :] = jnp.where(is_first, 0.0, cc_prev[:, 0:C_W].astype(F32))
    cbuf[HALO:HALO + TM, :] = cc_cur[:, 0:C_W].astype(F32)
    cbuf[HALO + TM:2 * HALO + TM, :] = jnp.where(is_last, 0.0, cc_next[:, 0:C_W].astype(F32))

    def conv_c_chunk(r):
        acc = jnp.zeros((rc, C_W), F32)
        for j in range(CONV_C):
            s = r * rc + HALO - (CONV_C - 1) // 2 + j
            acc = acc + ccw_ref[j:j + 1, :] * cbuf[s:s + rc, :]
        gate = cc_cur[r * rc:(r + 1) * rc, C_W:2 * C_W].astype(F32)
        mixed[r * rc:(r + 1) * rc, A_W + ATT_W:] = (gate * acc).astype(BF16)

    kvbuf[0:BLK, :] = kv_prev[...]
    kvbuf[BLK:BLK + TM, :] = kv_cur[...]
    kvbuf[BLK + TM:2 * BLK + TM, :] = kv_next[...]
    nqb = TM // BLK

    def attn_block(qb, kh):
        qs = jnp.concatenate(
            [q_ref[qb * BLK:(qb + 1) * BLK, (kh * GRP + g) * HEAD_DIM:(kh * GRP + g + 1) * HEAD_DIM]
             for g in range(GRP)], axis=0)
        kb = kvbuf[qb * BLK:qb * BLK + 3 * BLK, kh * HEAD_DIM:(kh + 1) * HEAD_DIM]
        vb = kvbuf[qb * BLK:qb * BLK + 3 * BLK, KV_W + kh * HEAD_DIM:KV_W + (kh + 1) * HEAD_DIM]
        s = lax.dot_general(qs, kb, (((1,), (1,)), ((), ())), preferred_element_type=F32)
        variant = (jnp.where(is_first, 1, 0) if qb == 0 else 0) + (jnp.where(is_last, 2, 0) if qb == nqb - 1 else 0)
        s = s + bias_ref[variant, kh]
        sk = sink_ref[kh]
        m = jnp.maximum(jnp.max(s, axis=-1, keepdims=True), sk)
        p = jnp.exp2(s - m)
        den = jnp.sum(p, axis=-1, keepdims=True) + jnp.exp2(sk - m)
        o = jnp.dot(p.astype(BF16), vb, preferred_element_type=F32) / den
        for g in range(GRP):
            c0 = A_W + (kh * GRP + g) * HEAD_DIM
            mixed[qb * BLK:(qb + 1) * BLK, c0:c0 + HEAD_DIM] = o[g * BLK:(g + 1) * BLK].astype(BF16)

    for qb in range(nqb):
        for kh in range(N_KV):
            attn_block(qb, kh)
    for r in range(TM // rc):
        conv_a_chunk(r)
    for r in range(TM // rc):
        conv_c_chunk(r)

    mix = jnp.dot(mixed[...], wout_ref[...], preferred_element_type=F32)
    if layer0:
        x = jnp.where(i < npt, xp_ref[...], xs_ref[...])
    else:
        x = x_ref[...]
    xmid = x + mod_ref[:, 2 * D:3 * D] * mix
    xmid_out[...] = xmid
    h2 = _rms_mod(xmid, gf_ref[...], mod_ref[:, 4 * D:5 * D], mod_ref[:, 3 * D:4 * D])

    lt = lax.dot_general(wrt_ref[...], h2.astype(BF16), (((1,), (1,)), ((), ())),
                         preferred_element_type=F32) + br_ref[...]
    bucket, w_a, w_b = _route_rows(lt)
    pay = jnp.concatenate([w_a, w_b, jnp.zeros((128 - 2, TM), F32)], axis=0).T
    _tile_to_rows(h2r_out, h2, TM, extra=pay)
    rows = lax.broadcasted_iota(jnp.int32, (RROWS, TM), 0).astype(F32)
    oh = jnp.where(rows == bucket, 1.0, 0.0)
    before = jnp.dot(oh.astype(BF16), tri_ref[...], preferred_element_type=F32)
    rank = jnp.sum(oh * (before + carry[:, 0:1]), axis=0, keepdims=True)
    carry[...] = carry[...] + jnp.sum(oh, axis=1, keepdims=True)
    cnt_out[...] = carry[...]
    zrow = jnp.zeros((ROUTE_ROWS - 2, TM), F32)
    route_out[...] = jnp.concatenate([bucket, rank, zrow], axis=0)


def _mix(layer, layer0, x_in, a, q, kv, cc, mod4, norm_ffn_g, caw, cab, lng, lnb, ccw, w_out_b,
         wrt, br, bias, sinkc, tri, T, npt, tps, tp, ss):
    nt = T // TM
    seq = lambda i: _seq_of_tile(i, npt, tps)
    row = lambda i: (i, 0)
    hp = TM // HALO
    bp = TM // BLK
    prev = lambda per: (lambda i: (jnp.maximum(i * per - 1, 0), 0))
    nxt = lambda per, n: (lambda i: (jnp.minimum((i + 1) * per, n - 1), 0))
    const2 = lambda i: (0, 0)
    lsel = lambda i: (layer, 0, 0)
    if layer0:
        nst = x_in[1].shape[0] // TM
        x_specs = [pl.BlockSpec((TM, D), lambda i: (jnp.minimum(i, npt - 1), 0)),
                   pl.BlockSpec((TM, D), lambda i: (jnp.clip(i - npt, 0, nst - 1), 0))]
    else:
        x_specs = [pl.BlockSpec((TM, D), row)]
    in_specs = x_specs + [
        pl.BlockSpec((TM, A_W), row),
        pl.BlockSpec((HALO, A_W), prev(hp)), pl.BlockSpec((HALO, A_W), nxt(hp, T // HALO)),
        pl.BlockSpec((TM, ATT_W), row),
        pl.BlockSpec((TM, 2 * KV_W), row),
        pl.BlockSpec((BLK, 2 * KV_W), prev(bp)), pl.BlockSpec((BLK, 2 * KV_W), nxt(bp, T // BLK)),
        pl.BlockSpec((TM, 2 * C_W), row),
        pl.BlockSpec((HALO, 2 * C_W), prev(hp)), pl.BlockSpec((HALO, 2 * C_W), nxt(hp, T // HALO)),
        pl.BlockSpec((None, None, 1, 6 * D), lambda i: (layer, seq(i), 0, 0)),
        pl.BlockSpec((None, 1, D), lsel),
        pl.BlockSpec((None, CONV_A, 8, A_W), lambda i: (layer, 0, 0, 0)),
        pl.BlockSpec((None, 1, A_W), lsel), pl.BlockSpec((None, 1, A_W), lsel),
        pl.BlockSpec((None, 1, A_W), lsel),
        pl.BlockSpec((None, 8, C_W), lsel),
        _resident((None, D, D), lsel),
        pl.BlockSpec((None, RROWS, D), lsel),
        pl.BlockSpec((None, RROWS, 1), lsel),
        _resident((4, N_KV, GRP * BLK, 3 * BLK), lambda i: (0, 0, 0, 0)),
        pl.BlockSpec((None, N_KV, GRP * BLK, 1), lambda i: (layer, 0, 0, 0)),
        pl.BlockSpec((TM, TM), const2),
    ]
    out_shape = [jax.ShapeDtypeStruct((T, D), F32), jax.ShapeDtypeStruct((T * LP, 128), F32),
                 jax.ShapeDtypeStruct((ROUTE_ROWS, T), F32), jax.ShapeDtypeStruct((RROWS, 128), F32)]
    out_specs = [pl.BlockSpec((TM, D), row), pl.BlockSpec((TM * LP, 128), row),
                 pl.BlockSpec((ROUTE_ROWS, TM), lambda i: (0, i)), pl.BlockSpec((RROWS, 128), const2)]
    scratch = [pltpu.VMEM((8, TM + 2 * HALO, A_W), F32), pltpu.VMEM((TM + 2 * HALO, C_W), F32),
               pltpu.VMEM((TM + 2 * BLK, 2 * KV_W), BF16), pltpu.VMEM((TM, D), BF16),
               pltpu.VMEM((RROWS, 128), F32)]
    args = list(x_in) + [a, a, a, q, kv, kv, kv, cc, cc, cc, mod4, norm_ffn_g, caw, cab, lng, lnb, ccw,
                         w_out_b, wrt, br, bias, sinkc, tri]
    return pl.pallas_call(
        functools.partial(_mix_body, layer0, npt, tp, ss),
        grid=(nt,), in_specs=in_specs, out_specs=out_specs, out_shape=out_shape,
        scratch_shapes=scratch, compiler_params=_cparams(),
    )(*args)


def _invert_body(pos_ref, lo_ref, hi_ref, src_ref):
    def pad_bucket(b, c):
        def zero(k, c2):
            src_ref[k] = 0
            return c2
        return lax.fori_loop(lo_ref[b], hi_ref[b], zero, c)

    lax.fori_loop(0, lo_ref.shape[0], pad_bucket, 0)

    def put(t, c):
        src_ref[pos_ref[t]] = t
        return c

    lax.fori_loop(0, pos_ref.shape[0], put, 0, unroll=32)


def _invert(pos, pad_lo, pad_hi, n_pad):
    return pl.pallas_call(
        _invert_body,
        grid_spec=pltpu.PrefetchScalarGridSpec(
            num_scalar_prefetch=3, grid=(1,), in_specs=[],
            out_specs=pl.BlockSpec(memory_space=pltpu.SMEM)),
        out_shape=jax.ShapeDtypeStruct((n_pad,), jnp.int32),
    )(pos, pad_lo, pad_hi)


def _expert_half(x, wg_ref, wu_ref, wd_ref):
    gate = jnp.dot(x, wg_ref[...], preferred_element_type=F32)
    he = gate * jax.nn.sigmoid(gate) * jnp.dot(x, wu_ref[...], preferred_element_type=F32)
    return jnp.dot(he.astype(BF16), wd_ref[...], preferred_element_type=F32)


def _expert_body(ea_ref, eb_ref, nv_ref, src_ref, h_hbm, wg_a, wu_a, wd_a, wg_b, wu_b, wd_b, y_ref, xbuf, xsem):
    j = pl.program_id(0)
    nv = nv_ref[j]
    slot = _gather_tile(src_ref, h_hbm, xbuf, xsem, TME, lambda step: nv_ref[step] > 0)

    @pl.when(nv > 0)
    def _():
        rows = xbuf.at[slot]
        x = _rows_to_tile(rows, TME).astype(BF16)
        w = rows[pl.ds(LT, TME, stride=LP), :]
        y = w[:, 0:1] * _expert_half(x, wg_a, wu_a, wd_a) + w[:, 1:2] * _expert_half(x, wg_b, wu_b, wd_b)
        _tile_to_rows(y_ref, y, TME)

    @pl.when(nv == 0)
    def _():
        y_ref[...] = jnp.zeros_like(y_ref)


def _experts(layer, tile_ea, tile_eb, tile_nv, src, h2r, wg_b, wu_b, wd_b, n_pad):
    nt = n_pad // TME
    wsel = lambda ref_idx: (lambda j, ea, eb, nv, src: (layer, (ea, eb)[ref_idx][j], 0, 0))
    up = lambda k: pl.BlockSpec((None, None, D, D_EXP), wsel(k))
    down = lambda k: pl.BlockSpec((None, None, D_EXP, D), wsel(k))
    return pl.pallas_call(
        _expert_body,
        grid_spec=pltpu.PrefetchScalarGridSpec(
            num_scalar_prefetch=4, grid=(nt,),
            in_specs=[pl.BlockSpec(memory_space=pl.ANY), up(0), up(0), down(0), up(1), up(1), down(1)],
            out_specs=pl.BlockSpec((TME * LP, 128), lambda j, ea, eb, nv, src: (j, 0)),
            scratch_shapes=_ring_scratch(TME)),
        out_shape=jax.ShapeDtypeStruct((n_pad * LP, 128), F32),
        compiler_params=_cparams(),
    )(tile_ea, tile_eb, tile_nv, src, h2r, wg_b, wu_b, wd_b, wg_b, wu_b, wd_b)


def _final_body(tile0, pos_ref, xm_ref, ys_ref, mod_ref, g_ref, o_ref, ybuf, ysem):
    slot = _gather_tile(pos_ref, ys_ref, ybuf, ysem, TM, lambda step: step >= 0, base=tile0 * TM)
    x = xm_ref[...] + mod_ref[:, 5 * D:6 * D] * _rows_to_tile(ybuf.at[slot], TM)
    o_ref[...] = x * lax.rsqrt(jnp.mean(x * x, axis=-1, keepdims=True) + EPS) * g_ref[...]


def _final(layer, pos, xmid, ysort, mod4, final_g, tile0, ntiles, npt, tps):
    seq = lambda i: _seq_of_tile(i + tile0, npt, tps)
    return pl.pallas_call(
        functools.partial(_final_body, tile0),
        grid_spec=pltpu.PrefetchScalarGridSpec(
            num_scalar_prefetch=1, grid=(ntiles,),
            in_specs=[pl.BlockSpec((TM, D), lambda i, pos: (i + tile0, 0)),
                      pl.BlockSpec(memory_space=pl.ANY),
                      pl.BlockSpec((None, None, 1, 6 * D), lambda i, pos: (layer, seq(i), 0, 0)),
                      pl.BlockSpec((1, D), lambda i, pos: (0, 0))],
            out_specs=pl.BlockSpec((TM, D), lambda i, pos: (i, 0)),
            scratch_shapes=_ring_scratch(TM)),
        out_shape=jax.ShapeDtypeStruct((ntiles * TM, D), F32),
        compiler_params=_cparams(),
    )(pos, xmid, ysort, mod4, final_g)


def _bucket_tables():
    ea, eb = [], []
    for g in range(N_GROUPS):
        for a in range(EPG):
            for b in range(a + 1, EPG):
                ea.append(g * EPG + a)
                eb.append(g * EPG + b)
    return np.asarray(ea, np.int32), np.asarray(eb, np.int32)


def _plan(route, counts, n_pad):
    bucket = route[0].astype(jnp.int32)
    rank = route[1].astype(jnp.int32)
    cnt = counts[:N_BUCKET, 0].astype(jnp.int32)
    padded = (cnt + TME - 1) // TME * TME
    ends = jnp.cumsum(padded)
    starts = ends - padded
    pos = starts[bucket] + rank
    nt = n_pad // TME
    tile_start = jnp.arange(nt, dtype=jnp.int32) * TME
    n_used = ends[-1] // TME
    tb = jnp.sum(tile_start[:, None] >= ends[None, :], axis=1).astype(jnp.int32)
    last_b = jnp.max(jnp.where(cnt > 0, jnp.arange(N_BUCKET, dtype=jnp.int32), 0))
    used = jnp.arange(nt, dtype=jnp.int32) < n_used
    tb = jnp.where(used, jnp.minimum(tb, N_BUCKET - 1), last_b)
    ea_tab, eb_tab = _bucket_tables()
    tile_ea = jnp.asarray(ea_tab)[tb]
    tile_eb = jnp.asarray(eb_tab)[tb]
    tile_nv = jnp.where(used, jnp.clip(starts[tb] + cnt[tb] - tile_start, 0, TME), 0).astype(jnp.int32)
    pad_lo = jnp.concatenate([starts + cnt, ends[-1:]]).astype(jnp.int32)
    pad_hi = jnp.concatenate([ends, jnp.full((1,), n_pad, jnp.int32)]).astype(jnp.int32)
    return pos, tile_ea, tile_eb, tile_nv, pad_lo, pad_hi


def _attn_bias():
    r = np.arange(BLK)[:, None]
    c = np.arange(3 * BLK)[None, :]
    dist = np.abs(c - BLK - r).astype(np.float32)
    slopes = 2.0 ** (-8.0 * np.arange(1, N_HEADS + 1, dtype=np.float32) / N_HEADS)
    tab = np.where(dist[None] <= WINDOW, -LOG2E * slopes[:, None, None] * dist[None], NEG).astype(np.float32)
    tabs = np.stack([np.where((c < BLK) & bool(v & 1) | (c >= 2 * BLK) & bool(v & 2), NEG, tab) for v in range(4)])
    return jnp.asarray(tabs.astype(np.float32).reshape(4, N_KV, GRP * BLK, 3 * BLK))


def kernel(x_prompt, x_sample, c_prompt, c_sample, norm_mix_g, norm_ffn_g, w_ada, b_ada, w_in, w_out,
           conv_a_w, conv_a_b, ln_a_g, ln_a_b, attn_sink, conv_c_w, w_router_group, b_router_group,
           w_router_expert, b_router_expert, w_gate, w_up, w_down, final_norm_g):
    depth = w_in.shape[0]
    bp, sp, _ = x_prompt.shape
    bs, ss, _ = x_sample.shape
    assert bp == 1 and sp % TM == 0 and ss % TM == 0 and (bp + bs) <= 8
    tp = bp * sp
    T = tp + bs * ss
    npt, tps = tp // TM, ss // TM
    n_pad = T + N_BUCKET * TME

    xp = x_prompt.reshape(tp, D)
    xs = x_sample.reshape(bs * ss, D)
    c_all = jnp.concatenate([c_prompt, c_sample, jnp.zeros((8 - bp - bs, D), F32)], axis=0)
    mod4 = _ada(c_all, w_ada, b_ada).reshape(depth, 8, 1, 6 * D)

    w_in_b = w_in.astype(BF16)
    w_out_b = w_out.astype(BF16)
    wg_b = w_gate.astype(BF16)
    wu_b = w_up.astype(BF16)
    wd_b = w_down.astype(BF16)
    wrt = jnp.concatenate([w_router_group, w_router_expert,
                           jnp.zeros((depth, D, RROWS - N_GROUPS - N_EXP), F32)], axis=-1)
    wrt = jnp.swapaxes(wrt, 1, 2).astype(BF16)
    br = jnp.concatenate([b_router_group, b_router_expert,
                          jnp.zeros((depth, RROWS - N_GROUPS - N_EXP), F32)], axis=-1)[..., None]
    caw = jnp.broadcast_to(conv_a_w[:, :, None, :], (depth, CONV_A, 8, A_W))
    ccw = jnp.pad(conv_c_w, ((0, 0), (0, 8 - CONV_C), (0, 0)))
    r3 = lambda v: v.reshape(depth, 1, -1)
    sinkc = jnp.broadcast_to((LOG2E * attn_sink).reshape(depth, N_KV, GRP, 1, 1),
                             (depth, N_KV, GRP, BLK, 1)).reshape(depth, N_KV, GRP * BLK, 1)
    bias = _attn_bias()
    tri = jnp.asarray(np.triu(np.ones((TM, TM), np.float32), 1)).astype(BF16)

    x_mid = ysort = pos = None
    for l in range(depth):
        if l == 0:
            a, q, kv, cc = _pre(l, False, (xp, xs), mod4, r3(norm_mix_g), w_in_b, T, npt, tps)
            x_in = (xp, xs)
        else:
            x, a, q, kv, cc = _pre(l, True, (x_mid, ysort), mod4, r3(norm_mix_g), w_in_b, T, npt, tps, pos)
            x_in = (x,)
        x_mid, h2r, route, counts = _mix(
            l, l == 0, x_in, a, q, kv, cc, mod4, r3(norm_ffn_g), caw, r3(conv_a_b), r3(ln_a_g),
            r3(ln_a_b), ccw, w_out_b, wrt, br, bias, sinkc, tri, T, npt, tps, tp, ss)
        pos, tile_ea, tile_eb, tile_nv, pad_lo, pad_hi = _plan(route, counts, n_pad)
        src = _invert(pos, pad_lo, pad_hi, n_pad)
        ysort = _experts(l, tile_ea, tile_eb, tile_nv, src, h2r, wg_b, wu_b, wd_b, n_pad)

    fg = final_norm_g.reshape(1, D)
    y_p = _final(depth - 1, pos, x_mid, ysort, mod4, fg, 0, npt, npt, tps)
    y_s = _final(depth - 1, pos, x_mid, ysort, mod4, fg, npt, bs * tps, npt, tps)
    return y_p.reshape(bp, sp, D), y_s.reshape(bs, ss, D)
```

```python
import functools

import jax
import jax.numpy as jnp
import numpy as np
from jax import lax
from jax.experimental import pallas as pl
from jax.experimental.pallas import tpu as pltpu

F32 = jnp.float32
BF16 = jnp.bfloat16

D = 2048
N_HEADS = 8
HEAD_DIM = 128
N_KV = 2
GRP = N_HEADS // N_KV
A_W = 512
C_W = 512
ATT_W = N_HEADS * HEAD_DIM
KV_W = N_KV * HEAD_DIM
D_IN = 4096
CONV_A = 31
CONV_C = 3
WINDOW = 128
BLK = 128
N_GROUPS = 4
EPG = 4
N_EXP = N_GROUPS * EPG
D_EXP = 512
N_PAIR = 6
N_BUCKET = N_GROUPS * N_PAIR
EPS = 1e-6
NEG = -1e30
LOG2E = 1.4426950408889634

TM = 256
TME = 256
HALO = 16
LT = D // 128
LP = LT + 1
ROUTE_ROWS = 8
RROWS = 32
VMEM_LIMIT = 56 * 1024 * 1024


def _cparams(sem=("arbitrary",)):
    return pltpu.CompilerParams(dimension_semantics=sem, vmem_limit_bytes=VMEM_LIMIT)


def _resident(shape, index_map):
    return pl.BlockSpec(shape, index_map, pipeline_mode=pl.Buffered(1))


def _ada_body(c_ref, w_ref, b_ref, o_ref):
    c = c_ref[...]
    sc = c * jax.nn.sigmoid(c)
    o_ref[...] = jnp.dot(sc, w_ref[...], preferred_element_type=F32,
                         precision=lax.Precision.HIGHEST) + b_ref[...]


def _ada(c_all, w_ada, b_ada):
    depth = w_ada.shape[0]
    tn = 1024
    return pl.pallas_call(
        _ada_body,
        grid=(depth, 6 * D // tn),
        in_specs=[pl.BlockSpec((8, D), lambda l, j: (0, 0)),
                  pl.BlockSpec((None, D, tn), lambda l, j: (l, 0, j)),
                  pl.BlockSpec((None, 1, tn), lambda l, j: (l, 0, j))],
        out_specs=pl.BlockSpec((None, 8, tn), lambda l, j: (l, 0, j)),
        out_shape=jax.ShapeDtypeStruct((depth, 8, 6 * D), F32),
        compiler_params=_cparams(("arbitrary", "arbitrary")),
    )(c_all, w_ada, b_ada.reshape(depth, 1, 6 * D))


def _seq_of_tile(i, npt, tps):
    return jnp.where(i < npt, 0, 1 + (i - npt) // tps)


def _rows_to_tile(ref, n):
    return jnp.concatenate([ref[pl.ds(c, n, stride=LP), :] for c in range(LT)], axis=1)


def _tile_to_rows(ref, val, n, extra=None):
    for c in range(LT):
        ref[pl.ds(c, n, stride=LP), :] = val[:, c * 128:(c + 1) * 128]
    ref[pl.ds(LT, n, stride=LP), :] = jnp.zeros((n, 128), F32) if extra is None else extra


def _row_dma(src, dst, s, d, sem):
    return pltpu.make_async_copy(src.at[pl.ds(s * LP, LP)], dst.at[pl.ds(d * LP, LP)], sem)


def _gather_tile(idx_ref, src_hbm, buf, sem, n, live, base=0):
    i = pl.program_id(0)
    slot = i % 2

    def start(step, s):
        @pl.when(live(step))
        def _():
            def issue(r, c):
                _row_dma(src_hbm, buf.at[s], idx_ref[base + step * n + r], r, sem.at[s]).start()
                return c
            lax.fori_loop(0, n, issue, 0, unroll=16)

    @pl.when(i == 0)
    def _():
        start(i, slot)

    @pl.when(i + 1 < pl.num_programs(0))
    def _():
        start(i + 1, 1 - slot)

    @pl.when(live(i))
    def _():
        pltpu.make_async_copy(src_hbm.at[pl.ds(0, n * LP)], buf.at[slot], sem.at[slot]).wait()

    return slot


def _rms_mod(x, g, scale, shift):
    y = x * lax.rsqrt(jnp.mean(x * x, axis=-1, keepdims=True) + EPS) * g
    return y * (1.0 + scale) + shift


def _pre_body(fused, npt, *refs):
    if fused:
        (pos_ref, xm_ref, ys_ref, modp_ref, mod_ref, g_ref, w_ref,
         x_out, a_out, q_out, kv_out, cc_out, ybuf, ysem) = refs
        slot = _gather_tile(pos_ref, ys_ref, ybuf, ysem, TM, lambda step: step >= 0)
        x = xm_ref[...] + modp_ref[:, 5 * D:6 * D] * _rows_to_tile(ybuf.at[slot], TM)
        x_out[...] = x
    else:
        xp_ref, xs_ref, mod_ref, g_ref, w_ref, a_out, q_out, kv_out, cc_out = refs
        x = jnp.where(pl.program_id(0) < npt, xp_ref[...], xs_ref[...])
    h = _rms_mod(x, g_ref[...], mod_ref[:, D:2 * D], mod_ref[:, 0:D]).astype(BF16)

    za = jnp.dot(h, w_ref[:, 0:2 * A_W], preferred_element_type=F32)
    a_out[...] = (za[:, :A_W] * jax.nn.sigmoid(za[:, A_W:])).astype(BF16)
    o = 2 * A_W
    zq = jnp.dot(h, w_ref[:, o:o + ATT_W], preferred_element_type=F32)
    q_out[...] = (zq * (LOG2E * HEAD_DIM ** -0.5)).astype(BF16)
    o += ATT_W
    kv_out[...] = jnp.dot(h, w_ref[:, o:o + 2 * KV_W], preferred_element_type=F32).astype(BF16)
    o += 2 * KV_W
    zc = jnp.dot(h, w_ref[:, o:o + 3 * C_W], preferred_element_type=F32)
    cc_out[:, 0:C_W] = (zc[:, 2 * C_W:] * zc[:, :C_W]).astype(BF16)
    cc_out[:, C_W:] = zc[:, C_W:2 * C_W].astype(BF16)


def _ring_scratch(n):
    return [pltpu.VMEM((2, n * LP, 128), F32), pltpu.SemaphoreType.DMA((2,))]


def _pre(layer, fused, xs_in, mod4, norm_g, w_in_b, T, npt, tps, pos=None):
    nt = T // TM
    seq = lambda i: _seq_of_tile(i, npt, tps)
    row = lambda i, *_: (i, 0)
    mod_spec = lambda l: pl.BlockSpec((None, None, 1, 6 * D), lambda i, *_: (l, seq(i), 0, 0))
    if fused:
        x_specs = [pl.BlockSpec((TM, D), row), pl.BlockSpec(memory_space=pl.ANY), mod_spec(layer - 1)]
    else:
        nst = xs_in[1].shape[0] // TM
        x_specs = [pl.BlockSpec((TM, D), lambda i: (jnp.minimum(i, npt - 1), 0)),
                   pl.BlockSpec((TM, D), lambda i: (jnp.clip(i - npt, 0, nst - 1), 0))]
    in_specs = x_specs + [mod_spec(layer),
                          pl.BlockSpec((None, 1, D), lambda i, *_: (layer, 0, 0)),
                          _resident((None, D, D_IN), lambda i, *_: (layer, 0, 0))]
    out_shape = [jax.ShapeDtypeStruct((T, A_W), BF16), jax.ShapeDtypeStruct((T, ATT_W), BF16),
                 jax.ShapeDtypeStruct((T, 2 * KV_W), BF16), jax.ShapeDtypeStruct((T, 2 * C_W), BF16)]
    out_specs = [pl.BlockSpec((TM, A_W), row), pl.BlockSpec((TM, ATT_W), row),
                 pl.BlockSpec((TM, 2 * KV_W), row), pl.BlockSpec((TM, 2 * C_W), row)]
    if fused:
        out_shape = [jax.ShapeDtypeStruct((T, D), F32)] + out_shape
        out_specs = [pl.BlockSpec((TM, D), row)] + out_specs
    args = ([pos] if fused else []) + list(xs_in) + ([mod4] if fused else []) + [mod4, norm_g, w_in_b]
    return pl.pallas_call(
        functools.partial(_pre_body, fused, npt),
        grid_spec=pltpu.PrefetchScalarGridSpec(
            num_scalar_prefetch=1 if fused else 0, grid=(nt,), in_specs=in_specs, out_specs=out_specs,
            scratch_shapes=_ring_scratch(TM) if fused else []),
        out_shape=out_shape, compiler_params=_cparams(),
    )(*args)


def _route_rows(lt):
    lg = [lt[r:r + 1, :] for r in range(N_GROUPS)]
    mg = functools.reduce(jnp.maximum, lg)
    gp = 1.0 / functools.reduce(jnp.add, [jnp.exp(l - mg) for l in lg])
    g = jnp.where(lg[0] == mg, 0.0, jnp.where(lg[1] == mg, 1.0, jnp.where(lg[2] == mg, 2.0, 3.0)))
    le = [lt[N_GROUPS + e:N_GROUPS + e + 1, :] for e in range(N_EXP)]
    sel = [jnp.where(g == 0.0, le[j], jnp.where(g == 1.0, le[EPG + j],
                                                jnp.where(g == 2.0, le[2 * EPG + j], le[3 * EPG + j])))
           for j in range(EPG)]
    ms = functools.reduce(jnp.maximum, sel)
    es = [jnp.exp(s - ms) for s in sel]
    zs = functools.reduce(jnp.add, es)
    p = [e / zs for e in es]
    v1 = functools.reduce(jnp.maximum, p)
    i1 = jnp.where(p[0] == v1, 0.0, jnp.where(p[1] == v1, 1.0, jnp.where(p[2] == v1, 2.0, 3.0)))
    p2 = [jnp.where(i1 == float(j), -1.0, p[j]) for j in range(EPG)]
    v2 = functools.reduce(jnp.maximum, p2)
    i2 = jnp.where(p2[0] == v2, 0.0, jnp.where(p2[1] == v2, 1.0, jnp.where(p2[2] == v2, 2.0, 3.0)))
    den = v1 + v2
    first = i1 < i2
    a = jnp.where(first, i1, i2)
    b = jnp.where(first, i2, i1)
    w_a = gp * (jnp.where(first, v1, v2) / den)
    w_b = gp * (jnp.where(first, v2, v1) / den)
    pair = jnp.where(a == 0.0, b - 1.0, jnp.where(a == 1.0, b + 1.0, 5.0))
    return g * float(N_PAIR) + pair, w_a, w_b


def _mix_body(layer0, npt, tp, ss, *refs):
    if layer0:
        xp_ref, xs_ref = refs[0], refs[1]
        refs = refs[2:]
    else:
        x_ref = refs[0]
        refs = refs[1:]
    (a_cur, a_prev, a_next, q_ref, kv_cur, kv_prev, kv_next, cc_cur, cc_prev, cc_next,
     mod_ref, gf_ref, caw_ref, cab_ref, lng_ref, lnb_ref, ccw_ref, wout_ref, wrt_ref, br_ref,
     bias_ref, sink_ref, tri_ref,
     xmid_out, h2r_out, route_out, cnt_out,
     abuf, cbuf, kvbuf, mixed, carry) = refs

    i = pl.program_id(0)
    t0 = i * TM
    in_p = t0 < tp
    p0 = jnp.where(in_p, t0, (t0 - tp) % ss)
    slen = jnp.where(in_p, tp, ss)
    is_first = p0 == 0
    is_last = p0 + TM == slen

    @pl.when(i == 0)
    def _():
        carry[...] = jnp.zeros_like(carry)

    abuf[0, 0:HALO, :] = jnp.where(is_first, 0.0, a_prev[...].astype(F32))
    abuf[0, HALO:HALO + TM, :] = a_cur[...].astype(F32)
    abuf[0, HALO + TM:2 * HALO + TM, :] = jnp.where(is_last, 0.0, a_next[...].astype(F32))
    for k in range(1, 8):
        abuf[k, 0:TM + 24, :] = abuf[0, k:k + TM + 24, :]
    rc = 64

    def conv_a_chunk(r):
        acc = jnp.zeros((rc // 8, 8, A_W), F32)
        for j in range(CONV_A):
            s = HALO - (CONV_A - 1) // 2 + j
            o = r * rc + s // 8 * 8
            acc = acc + caw_ref[j] * abuf[s % 8, o:o + rc, :].reshape(rc // 8, 8, A_W)
        acc = acc.reshape(rc, A_W) + cab_ref[...]
        mu = jnp.mean(acc, axis=-1, keepdims=True)
        xc = acc - mu
        var = jnp.mean(xc * xc, axis=-1, keepdims=True)
        yn = xc * lax.rsqrt(var + EPS) * lng_ref[...] + lnb_ref[...]
        mixed[r * rc:(r + 1) * rc, 0:A_W] = (yn * jax.nn.sigmoid(yn)).astype(BF16)

    cbuf[0:HALO, :] = jnp.where(is_first, 0.0, cc_prev[:, 0:C_W].astype(F32))
    cbuf[HALO:HALO + TM, :] = cc_cur[:, 0:C_W].astype(F32)
    cbuf[HALO + TM:2 * HALO + TM, :] = jnp.where(is_last, 0.0, cc_next[:, 0:C_W].astype(F32))

    def conv_c_chunk(r):
        acc = jnp.zeros((rc, C_W), F32)
        for j in range(CONV_C):
            s = r * rc + HALO - (CONV_C - 1) // 2 + j
            acc = acc + ccw_ref[j:j + 1, :] * cbuf[s:s + rc, :]
        gate = cc_cur[r * rc:(r + 1) * rc, C_W:2 * C_W].astype(F32)
        mixed[r * rc:(r + 1) * rc, A_W + ATT_W:] = (gate * acc).astype(BF16)

    kvbuf[0:BLK, :] = kv_prev[...]
    kvbuf[BLK:BLK + TM, :] = kv_cur[...]
    kvbuf[BLK + TM:2 * BLK + TM, :] = kv_next[...]
    col = lax.broadcasted_iota(jnp.int32, (1, 3 * BLK), 1)
    pen_first = jnp.where(jnp.logical_and(is_first, col < BLK), NEG, 0.0)
    pen_last = jnp.where(jnp.logical_and(is_last, col >= 2 * BLK), NEG, 0.0)
    nqb = TM // BLK

    def attn_block(qb, kh):
        qs = jnp.concatenate(
            [q_ref[qb * BLK:(qb + 1) * BLK, (kh * GRP + g) * HEAD_DIM:(kh * GRP + g + 1) * HEAD_DIM]
             for g in range(GRP)], axis=0)
        kb = kvbuf[qb * BLK:qb * BLK + 3 * BLK, kh * HEAD_DIM:(kh + 1) * HEAD_DIM]
        vb = kvbuf[qb * BLK:qb * BLK + 3 * BLK, KV_W + kh * HEAD_DIM:KV_W + (kh + 1) * HEAD_DIM]
        s = lax.dot_general(qs, kb, (((1,), (1,)), ((), ())), preferred_element_type=F32)
        s = s + bias_ref[kh]
        if qb == 0:
            s = s + pen_first
        if qb == nqb - 1:
            s = s + pen_last
        sk = sink_ref[kh]
        m = jnp.maximum(jnp.max(s, axis=-1, keepdims=True), sk)
        p = jnp.exp2(s - m)
        den = jnp.sum(p, axis=-1, keepdims=True) + jnp.exp2(sk - m)
        o = jnp.dot(p.astype(BF16), vb, preferred_element_type=F32) / den
        for g in range(GRP):
            c0 = A_W + (kh * GRP + g) * HEAD_DIM
            mixed[qb * BLK:(qb + 1) * BLK, c0:c0 + HEAD_DIM] = o[g * BLK:(g + 1) * BLK].astype(BF16)

    blocks = [(qb, kh) for qb in range(nqb) for kh in range(N_KV)]
    for r in range(TM // rc):
        attn_block(*blocks[r])
        conv_a_chunk(r)
    for k in range(TM // rc, len(blocks)):
        attn_block(*blocks[k])
    for r in range(TM // rc):
        conv_c_chunk(r)

    mix = jnp.dot(mixed[...], wout_ref[...], preferred_element_type=F32)
    if layer0:
        x = jnp.where(i < npt, xp_ref[...], xs_ref[...])
    else:
        x = x_ref[...]
    xmid = x + mod_ref[:, 2 * D:3 * D] * mix
    xmid_out[...] = xmid
    h2 = _rms_mod(xmid, gf_ref[...], mod_ref[:, 4 * D:5 * D], mod_ref[:, 3 * D:4 * D])

    lt = lax.dot_general(wrt_ref[...], h2.astype(BF16), (((1,), (1,)), ((), ())),
                         preferred_element_type=F32) + br_ref[...]
    bucket, w_a, w_b = _route_rows(lt)
    pay = jnp.concatenate([w_a, w_b, jnp.zeros((128 - 2, TM), F32)], axis=0).T
    _tile_to_rows(h2r_out, h2, TM, extra=pay)
    rows = lax.broadcasted_iota(jnp.int32, (RROWS, TM), 0).astype(F32)
    oh = jnp.where(rows == bucket, 1.0, 0.0)
    before = jnp.dot(oh.astype(BF16), tri_ref[...], preferred_element_type=F32)
    rank = jnp.sum(oh * (before + carry[:, 0:1]), axis=0, keepdims=True)
    carry[...] = carry[...] + jnp.sum(oh, axis=1, keepdims=True)
    cnt_out[...] = carry[...]
    zrow = jnp.zeros((ROUTE_ROWS - 2, TM), F32)
    route_out[...] = jnp.concatenate([bucket, rank, zrow], axis=0)


def _mix(layer, layer0, x_in, a, q, kv, cc, mod4, norm_ffn_g, caw, cab, lng, lnb, ccw, w_out_b,
         wrt, br, bias, sinkc, tri, T, npt, tps, tp, ss):
    nt = T // TM
    seq = lambda i: _seq_of_tile(i, npt, tps)
    row = lambda i: (i, 0)
    hp = TM // HALO
    bp = TM // BLK
    prev = lambda per: (lambda i: (jnp.maximum(i * per - 1, 0), 0))
    nxt = lambda per, n: (lambda i: (jnp.minimum((i + 1) * per, n - 1), 0))
    const2 = lambda i: (0, 0)
    lsel = lambda i: (layer, 0, 0)
    if layer0:
        nst = x_in[1].shape[0] // TM
        x_specs = [pl.BlockSpec((TM, D), lambda i: (jnp.minimum(i, npt - 1), 0)),
                   pl.BlockSpec((TM, D), lambda i: (jnp.clip(i - npt, 0, nst - 1), 0))]
    else:
        x_specs = [pl.BlockSpec((TM, D), row)]
    in_specs = x_specs + [
        pl.BlockSpec((TM, A_W), row),
        pl.BlockSpec((HALO, A_W), prev(hp)), pl.BlockSpec((HALO, A_W), nxt(hp, T // HALO)),
        pl.BlockSpec((TM, ATT_W), row),
        pl.BlockSpec((TM, 2 * KV_W), row),
        pl.BlockSpec((BLK, 2 * KV_W), prev(bp)), pl.BlockSpec((BLK, 2 * KV_W), nxt(bp, T // BLK)),
        pl.BlockSpec((TM, 2 * C_W), row),
        pl.BlockSpec((HALO, 2 * C_W), prev(hp)), pl.BlockSpec((HALO, 2 * C_W), nxt(hp, T // HALO)),
        pl.BlockSpec((None, None, 1, 6 * D), lambda i: (layer, seq(i), 0, 0)),
        pl.BlockSpec((None, 1, D), lsel),
        pl.BlockSpec((None, CONV_A, 8, A_W), lambda i: (layer, 0, 0, 0)),
        pl.BlockSpec((None, 1, A_W), lsel), pl.BlockSpec((None, 1, A_W), lsel),
        pl.BlockSpec((None, 1, A_W), lsel),
        pl.BlockSpec((None, 8, C_W), lsel),
        _resident((None, D, D), lsel),
        pl.BlockSpec((None, RROWS, D), lsel),
        pl.BlockSpec((None, RROWS, 1), lsel),
        pl.BlockSpec((N_KV, GRP * BLK, 3 * BLK), lambda i: (0, 0, 0)),
        pl.BlockSpec((None, N_KV, GRP * BLK, 1), lambda i: (layer, 0, 0, 0)),
        pl.BlockSpec((TM, TM), const2),
    ]
    out_shape = [jax.ShapeDtypeStruct((T, D), F32), jax.ShapeDtypeStruct((T * LP, 128), F32),
                 jax.ShapeDtypeStruct((ROUTE_ROWS, T), F32), jax.ShapeDtypeStruct((RROWS, 128), F32)]
    out_specs = [pl.BlockSpec((TM, D), row), pl.BlockSpec((TM * LP, 128), row),
                 pl.BlockSpec((ROUTE_ROWS, TM), lambda i: (0, i)), pl.BlockSpec((RROWS, 128), const2)]
    scratch = [pltpu.VMEM((8, TM + 2 * HALO, A_W), F32), pltpu.VMEM((TM + 2 * HALO, C_W), F32),
               pltpu.VMEM((TM + 2 * BLK, 2 * KV_W), BF16), pltpu.VMEM((TM, D), BF16),
               pltpu.VMEM((RROWS, 128), F32)]
    args = list(x_in) + [a, a, a, q, kv, kv, kv, cc, cc, cc, mod4, norm_ffn_g, caw, cab, lng, lnb, ccw,
                         w_out_b, wrt, br, bias, sinkc, tri]
    return pl.pallas_call(
        functools.partial(_mix_body, layer0, npt, tp, ss),
        grid=(nt,), in_specs=in_specs, out_specs=out_specs, out_shape=out_shape,
        scratch_shapes=scratch, compiler_params=_cparams(),
    )(*args)


def _invert_body(pos_ref, lo_ref, hi_ref, src_ref):
    def pad_bucket(b, c):
        def zero(k, c2):
            src_ref[k] = 0
            return c2
        return lax.fori_loop(lo_ref[b], hi_ref[b], zero, c)

    lax.fori_loop(0, lo_ref.shape[0], pad_bucket, 0)

    def put(t, c):
        src_ref[pos_ref[t]] = t
        return c

    lax.fori_loop(0, pos_ref.shape[0], put, 0, unroll=32)


def _invert(pos, pad_lo, pad_hi, n_pad):
    return pl.pallas_call(
        _invert_body,
        grid_spec=pltpu.PrefetchScalarGridSpec(
            num_scalar_prefetch=3, grid=(1,), in_specs=[],
            out_specs=pl.BlockSpec(memory_space=pltpu.SMEM)),
        out_shape=jax.ShapeDtypeStruct((n_pad,), jnp.int32),
    )(pos, pad_lo, pad_hi)


def _expert_half(x, wg_ref, wu_ref, wd_ref):
    gate = jnp.dot(x, wg_ref[...], preferred_element_type=F32)
    he = gate * jax.nn.sigmoid(gate) * jnp.dot(x, wu_ref[...], preferred_element_type=F32)
    return jnp.dot(he.astype(BF16), wd_ref[...], preferred_element_type=F32)


def _expert_body(ea_ref, eb_ref, nv_ref, src_ref, h_hbm, wg_a, wu_a, wd_a, wg_b, wu_b, wd_b, y_ref, xbuf, xsem):
    j = pl.program_id(0)
    nv = nv_ref[j]
    slot = _gather_tile(src_ref, h_hbm, xbuf, xsem, TME, lambda step: nv_ref[step] > 0)

    @pl.when(nv > 0)
    def _():
        rows = xbuf.at[slot]
        x = _rows_to_tile(rows, TME).astype(BF16)
        w = rows[pl.ds(LT, TME, stride=LP), :]
        y = w[:, 0:1] * _expert_half(x, wg_a, wu_a, wd_a) + w[:, 1:2] * _expert_half(x, wg_b, wu_b, wd_b)
        _tile_to_rows(y_ref, y, TME)

    @pl.when(nv == 0)
    def _():
        y_ref[...] = jnp.zeros_like(y_ref)


def _experts(layer, tile_ea, tile_eb, tile_nv, src, h2r, wg_b, wu_b, wd_b, n_pad):
    nt = n_pad // TME
    wsel = lambda ref_idx: (lambda j, ea, eb, nv, src: (layer, (ea, eb)[ref_idx][j], 0, 0))
    up = lambda k: pl.BlockSpec((None, None, D, D_EXP), wsel(k))
    down = lambda k: pl.BlockSpec((None, None, D_EXP, D), wsel(k))
    return pl.pallas_call(
        _expert_body,
        grid_spec=pltpu.PrefetchScalarGridSpec(
            num_scalar_prefetch=4, grid=(nt,),
            in_specs=[pl.BlockSpec(memory_space=pl.ANY), up(0), up(0), down(0), up(1), up(1), down(1)],
            out_specs=pl.BlockSpec((TME * LP, 128), lambda j, ea, eb, nv, src: (j, 0)),
            scratch_shapes=_ring_scratch(TME)),
        out_shape=jax.ShapeDtypeStruct((n_pad * LP, 128), F32),
        compiler_params=_cparams(),
    )(tile_ea, tile_eb, tile_nv, src, h2r, wg_b, wu_b, wd_b, wg_b, wu_b, wd_b)


def _final_body(tile0, pos_ref, xm_ref, ys_ref, mod_ref, g_ref, o_ref, ybuf, ysem):
    slot = _gather_tile(pos_ref, ys_ref, ybuf, ysem, TM, lambda step: step >= 0, base=tile0 * TM)
    x = xm_ref[...] + mod_ref[:, 5 * D:6 * D] * _rows_to_tile(ybuf.at[slot], TM)
    o_ref[...] = x * lax.rsqrt(jnp.mean(x * x, axis=-1, keepdims=True) + EPS) * g_ref[...]


def _final(layer, pos, xmid, ysort, mod4, final_g, tile0, ntiles, npt, tps):
    seq = lambda i: _seq_of_tile(i + tile0, npt, tps)
    return pl.pallas_call(
        functools.partial(_final_body, tile0),
        grid_spec=pltpu.PrefetchScalarGridSpec(
            num_scalar_prefetch=1, grid=(ntiles,),
            in_specs=[pl.BlockSpec((TM, D), lambda i, pos: (i + tile0, 0)),
                      pl.BlockSpec(memory_space=pl.ANY),
                      pl.BlockSpec((None, None, 1, 6 * D), lambda i, pos: (layer, seq(i), 0, 0)),
                      pl.BlockSpec((1, D), lambda i, pos: (0, 0))],
            out_specs=pl.BlockSpec((TM, D), lambda i, pos: (i, 0)),
            scratch_shapes=_ring_scratch(TM)),
        out_shape=jax.ShapeDtypeStruct((ntiles * TM, D), F32),
        compiler_params=_cparams(),
    )(pos, xmid, ysort, mod4, final_g)


def _bucket_tables():
    ea, eb = [], []
    for g in range(N_GROUPS):
        for a in range(EPG):
            for b in range(a + 1, EPG):
                ea.append(g * EPG + a)
                eb.append(g * EPG + b)
    return np.asarray(ea, np.int32), np.asarray(eb, np.int32)


def _plan(route, counts, n_pad):
    bucket = route[0].astype(jnp.int32)
    rank = route[1].astype(jnp.int32)
    cnt = counts[:N_BUCKET, 0].astype(jnp.int32)
    padded = (cnt + TME - 1) // TME * TME
    ends = jnp.cumsum(padded)
    starts = ends - padded
    pos = starts[bucket] + rank
    nt = n_pad // TME
    tile_start = jnp.arange(nt, dtype=jnp.int32) * TME
    n_used = ends[-1] // TME
    tb = jnp.sum(tile_start[:, None] >= ends[None, :], axis=1).astype(jnp.int32)
    last_b = jnp.max(jnp.where(cnt > 0, jnp.arange(N_BUCKET, dtype=jnp.int32), 0))
    used = jnp.arange(nt, dtype=jnp.int32) < n_used
    tb = jnp.where(used, jnp.minimum(tb, N_BUCKET - 1), last_b)
    ea_tab, eb_tab = _bucket_tables()
    tile_ea = jnp.asarray(ea_tab)[tb]
    tile_eb = jnp.asarray(eb_tab)[tb]
    tile_nv = jnp.where(used, jnp.clip(starts[tb] + cnt[tb] - tile_start, 0, TME), 0).astype(jnp.int32)
    pad_lo = jnp.concatenate([starts + cnt, ends[-1:]]).astype(jnp.int32)
    pad_hi = jnp.concatenate([ends, jnp.full((1,), n_pad, jnp.int32)]).astype(jnp.int32)
    return pos, tile_ea, tile_eb, tile_nv, pad_lo, pad_hi


def _attn_bias():
    r = np.arange(BLK)[:, None]
    c = np.arange(3 * BLK)[None, :]
    dist = np.abs(c - BLK - r).astype(np.float32)
    slopes = 2.0 ** (-8.0 * np.arange(1, N_HEADS + 1, dtype=np.float32) / N_HEADS)
    tab = np.where(dist[None] <= WINDOW, -LOG2E * slopes[:, None, None] * dist[None], NEG).astype(np.float32)
    return jnp.asarray(tab.reshape(N_KV, GRP * BLK, 3 * BLK))


def kernel(x_prompt, x_sample, c_prompt, c_sample, norm_mix_g, norm_ffn_g, w_ada, b_ada, w_in, w_out,
           conv_a_w, conv_a_b, ln_a_g, ln_a_b, attn_sink, conv_c_w, w_router_group, b_router_group,
           w_router_expert, b_router_expert, w_gate, w_up, w_down, final_norm_g):
    depth = w_in.shape[0]
    bp, sp, _ = x_prompt.shape
    bs, ss, _ = x_sample.shape
    assert bp == 1 and sp % TM == 0 and ss % TM == 0 and (bp + bs) <= 8
    tp = bp * sp
    T = tp + bs * ss
    npt, tps = tp // TM, ss // TM
    n_pad = T + N_BUCKET * TME

    xp = x_prompt.reshape(tp, D)
    xs = x_sample.reshape(bs * ss, D)
    c_all = jnp.concatenate([c_prompt, c_sample, jnp.zeros((8 - bp - bs, D), F32)], axis=0)
    mod4 = _ada(c_all, w_ada, b_ada).reshape(depth, 8, 1, 6 * D)

    w_in_b = w_in.astype(BF16)
    w_out_b = w_out.astype(BF16)
    wg_b = w_gate.astype(BF16)
    wu_b = w_up.astype(BF16)
    wd_b = w_down.astype(BF16)
    wrt = jnp.concatenate([w_router_group, w_router_expert,
                           jnp.zeros((depth, D, RROWS - N_GROUPS - N_EXP), F32)], axis=-1)
    wrt = jnp.swapaxes(wrt, 1, 2).astype(BF16)
    br = jnp.concatenate([b_router_group, b_router_expert,
                          jnp.zeros((depth, RROWS - N_GROUPS - N_EXP), F32)], axis=-1)[..., None]
    caw = jnp.broadcast_to(conv_a_w[:, :, None, :], (depth, CONV_A, 8, A_W))
    ccw = jnp.pad(conv_c_w, ((0, 0), (0, 8 - CONV_C), (0, 0)))
    r3 = lambda v: v.reshape(depth, 1, -1)
    sinkc = jnp.broadcast_to((LOG2E * attn_sink).reshape(depth, N_KV, GRP, 1, 1),
                             (depth, N_KV, GRP, BLK, 1)).reshape(depth, N_KV, GRP * BLK, 1)
    bias = _attn_bias()
    tri = jnp.asarray(np.triu(np.ones((TM, TM), np.float32), 1)).astype(BF16)

    x_mid = ysort = pos = None
    for l in range(depth):
        if l == 0:
            a, q, kv, cc = _pre(l, False, (xp, xs), mod4, r3(norm_mix_g), w_in_b, T, npt, tps)
            x_in = (xp, xs)
        else:
            x, a, q, kv, cc = _pre(l, True, (x_mid, ysort), mod4, r3(norm_mix_g), w_in_b, T, npt, tps, pos)
            x_in = (x,)
        x_mid, h2r, route, counts = _mix(
            l, l == 0, x_in, a, q, kv, cc, mod4, r3(norm_ffn_g), caw, r3(conv_a_b), r3(ln_a_g),
            r3(ln_a_b), ccw, w_out_b, wrt, br, bias, sinkc, tri, T, npt, tps, tp, ss)
        pos, tile_ea, tile_eb, tile_nv, pad_lo, pad_hi = _plan(route, counts, n_pad)
        src = _invert(pos, pad_lo, pad_hi, n_pad)
        ysort = _experts(l, tile_ea, tile_eb, tile_nv, src, h2r, wg_b, wu_b, wd_b, n_pad)

    fg = final_norm_g.reshape(1, D)
    y_p = _final(depth - 1, pos, x_mid, ysort, mod4, fg, 0, npt, npt, tps)
    y_s = _final(depth - 1, pos, x_mid, ysort, mod4, fg, npt, bs * tps, npt, tps)
    return y_p.reshape(bp, sp, D), y_s.reshape(bs, ss, D)
```
